```python
import jax
import jax.numpy as jnp
from jax import lax
import numpy as np

D_MODEL = 1024
BATCH = 2
SEQ = 16384
DEPTH = 4

N_MIXERS = 4
MIXER_ORDER = ('stick_breaking', 'conformer_conv', 'dsa_sparse', 'mla')
Q_BLOCK = 128
CAUSAL_SEGMENTS = 16
ROPE_THETA = 10000.0
LN_EPS = 1e-5
RMS_EPS = 1e-6
DEEPNORM_ALPHA = (2.0 * DEPTH) ** 0.25
DEEPNORM_BETA = (8.0 * DEPTH) ** -0.25

MIX_WIDTH = D_MODEL
SB_HEADS = 8
SB_HEAD_DIM = MIX_WIDTH // SB_HEADS
CONV_CH = MIX_WIDTH
CONV_WIDTH = 31
DSA_HEADS = 16
DSA_HEAD_DIM = MIX_WIDTH // DSA_HEADS
IDX_HEADS = 8
IDX_DIM = 64
IDX_TOPK_MAX = 256
MLA_HEADS = 8
MLA_Q_RANK = 384
MLA_KV_RANK = 256
MLA_NOPE = 64
MLA_ROPE = 32
MLA_V = MIX_WIDTH // MLA_HEADS
MEM_TOKENS = 256
MEM_HEADS = 4
MEM_HEAD_DIM = 64
MEM_WIDTH = MEM_HEADS * MEM_HEAD_DIM
OUT_WIDTH = MIX_WIDTH + MEM_WIDTH
N_EXPERTS = 16
N_GROUPS = 4
EXPERTS_PER_GROUP = N_EXPERTS // N_GROUPS
TOPK_GROUPS = 1
TOPK_EXPERTS = 2
EXPERT_FF = 512
MOE_CHUNK = 512

W_IN_SB = 3 * MIX_WIDTH + MEM_WIDTH
W_IN_CONV = 2 * CONV_CH + MEM_WIDTH
W_IN_DSA = MIX_WIDTH + 2 * DSA_HEAD_DIM + IDX_HEADS * IDX_DIM + IDX_DIM + IDX_HEADS + MEM_WIDTH
W_IN_MLA = MLA_Q_RANK + MLA_KV_RANK + MLA_ROPE + MEM_WIDTH

F32 = jnp.float32

kernel_name = 'hybrid_stickbreak_conv_dsa_mla_sharedrouter_moe'


def _split_cols(h, sizes):
    idx = np.cumsum(sizes)[:-1].tolist()
    return jnp.split(h, idx, axis=-1)


def _layer_norm(x, g, b):
    xf = x.astype(F32)
    mu = jnp.mean(xf, axis=-1, keepdims=True)
    var = jnp.mean(jnp.square(xf - mu), axis=-1, keepdims=True)
    y = (xf - mu) * lax.rsqrt(var + LN_EPS) * g.astype(F32) + b.astype(F32)
    return y.astype(x.dtype)


def _rms_norm(x, g):
    xf = x.astype(F32)
    y = xf * lax.rsqrt(jnp.mean(jnp.square(xf), axis=-1, keepdims=True) + RMS_EPS) * g.astype(F32)
    return y.astype(x.dtype)


def _rope(x, pos):
    half = x.shape[-1] // 2
    inv_freq = ROPE_THETA ** (-jnp.arange(half, dtype=F32) / half)
    ang = pos.astype(F32)[..., None] * inv_freq
    cos = jnp.cos(ang)[:, :, None, :]
    sin = jnp.sin(ang)[:, :, None, :]
    xf = x.astype(F32)
    x1, x2 = xf[..., :half], xf[..., half:]
    return jnp.concatenate([x1 * cos - x2 * sin, x2 * cos + x1 * sin], axis=-1).astype(x.dtype)


def _to_blocks(a):
    a = a.reshape(a.shape[0], a.shape[1] // Q_BLOCK, Q_BLOCK, *a.shape[2:])
    return jnp.moveaxis(a, 1, 0)


def _from_blocks(a):
    a = jnp.moveaxis(a, 0, 1)
    return a.reshape(a.shape[0], a.shape[1] * a.shape[2], *a.shape[3:])


def _causal_sweep(block_fn, q_inputs, kv_inputs):
    seq = q_inputs[0].shape[1]
    n_blk = seq // Q_BLOCK
    n_seg = min(CAUSAL_SEGMENTS, n_blk)
    outs = []
    for j in range(n_seg):
        b0 = j * n_blk // n_seg
        b1 = (j + 1) * n_blk // n_seg
        k_len = b1 * Q_BLOCK
        kv = tuple(a[:, :k_len] for a in kv_inputs)
        qs = tuple(_to_blocks(a[:, b0 * Q_BLOCK:k_len]) for a in q_inputs)

        def body(args, kv=kv):
            return block_fn(*args, *kv)

        outs.append(_from_blocks(lax.map(body, (*qs, jnp.arange(b0, b1)))))
    return jnp.concatenate(outs, axis=1)


def _stick_breaking_attention(q, k, v):
    scale = q.shape[-1] ** -0.5

    def block(qb, blk, kf, vf):
        q_pos = blk * Q_BLOCK + jnp.arange(Q_BLOCK)
        key_pos = jnp.arange(kf.shape[1])
        z = jnp.einsum('bqhd,bkhd->bhqk', qb.astype(F32), kf.astype(F32)) * scale
        valid = key_pos[None, :] < q_pos[:, None]
        log_keep = jnp.where(valid, jax.nn.log_sigmoid(-z), 0.0)
        a = jnp.where(valid, jnp.exp(z + lax.cumsum(log_keep, axis=3, reverse=True)), 0.0)
        return jnp.einsum('bhqk,bkhd->bqhd', a, vf.astype(F32))

    return _causal_sweep(block, (q,), (k, v)).astype(v.dtype)


def _conformer_conv(a, g, conv_w, conv_b, norm_g, norm_b):
    h = a * jax.nn.sigmoid(g)
    hp = jnp.pad(h, ((0, 0), (CONV_WIDTH - 1, 0), (0, 0)))
    y = lax.conv_general_dilated(hp, conv_w[:, None, :].astype(h.dtype), window_strides=(1,), padding='VALID',
                                 dimension_numbers=('NWC', 'WIO', 'NWC'), feature_group_count=CONV_CH)
    y = y + conv_b.astype(y.dtype)
    return jax.nn.silu(_layer_norm(y, norm_g, norm_b))


def _dsa_attention(q, k, v, q_idx, k_idx, w_idx):
    seq = q.shape[1]
    n_sel_full = min(IDX_TOPK_MAX, seq // 4)
    scale = q.shape[-1] ** -0.5
    gather = jax.vmap(lambda src, ids: src[ids])

    def block(qb, qib, wb, blk, kk, vv, ki):
        k_len = kk.shape[1]
        n_sel = min(n_sel_full, k_len)
        q_pos = blk * Q_BLOCK + jnp.arange(Q_BLOCK)
        key_pos = jnp.arange(k_len)
        dots = jnp.einsum('bqhd,bkd->bqhk', qib.astype(F32), ki.astype(F32)) * (IDX_DIM ** -0.5)
        score = jnp.einsum('bqh,bqhk->bqk', wb.astype(F32) * (IDX_HEADS ** -0.5), jax.nn.relu(dots))
        score = jnp.where((key_pos[None, :] <= q_pos[:, None])[None], score, -jnp.inf)
        _, sel = lax.top_k(score, n_sel)
        sel_ok = sel <= q_pos[None, :, None]
        kg = gather(kk, sel).astype(F32)
        vg = gather(vv, sel).astype(F32)
        s = jnp.einsum('bqhd,bqkd->bhqk', qb.astype(F32), kg) * scale
        s = jnp.where(sel_ok[:, None], s, -jnp.inf)
        p = jax.nn.softmax(s, axis=-1)
        return jnp.einsum('bhqk,bqkd->bqhd', p, vg)

    return _causal_sweep(block, (q, q_idx, w_idx), (k, v, k_idx)).astype(v.dtype)


def _mla_attention(c_q, c_kv, k_rope, pos, q_norm_g, kv_norm_g, w_q_up, w_kv_up):
    bsz, seq = c_q.shape[:2]
    q = (_rms_norm(c_q, q_norm_g) @ w_q_up).reshape(bsz, seq, MLA_HEADS, MLA_NOPE + MLA_ROPE)
    q_nope, q_rope = q[..., :MLA_NOPE], _rope(q[..., MLA_NOPE:], pos)
    kv = (_rms_norm(c_kv, kv_norm_g) @ w_kv_up).reshape(bsz, seq, MLA_HEADS, MLA_NOPE + MLA_V)
    k_nope, v = kv[..., :MLA_NOPE], kv[..., MLA_NOPE:]
    kr = _rope(k_rope[:, :, None, :], pos)[:, :, 0]
    scale = (MLA_NOPE + MLA_ROPE) ** -0.5

    def block(qn, qr, blk, kn, krr, vf):
        q_pos = blk * Q_BLOCK + jnp.arange(Q_BLOCK)
        key_pos = jnp.arange(kn.shape[1])
        s = (jnp.einsum('bqhd,bkhd->bhqk', qn.astype(F32), kn.astype(F32))
             + jnp.einsum('bqhd,bkd->bhqk', qr.astype(F32), krr.astype(F32))) * scale
        s = jnp.where(key_pos[None, :] <= q_pos[:, None], s, -jnp.inf)
        p = jax.nn.softmax(s, axis=-1)
        return jnp.einsum('bhqk,bkhd->bqhd', p, vf.astype(F32))

    out = _causal_sweep(block, (q_nope, q_rope), (k_nope, kr, v))
    return out.reshape(bsz, seq, MLA_HEADS * MLA_V).astype(c_q.dtype)


def _memory_attention(mq, mem, w_mem_kv):
    bsz, seq = mq.shape[:2]
    q = mq.reshape(bsz, seq, MEM_HEADS, MEM_HEAD_DIM).astype(F32)
    kv = (mem @ w_mem_kv).reshape(bsz, mem.shape[1], 2, MEM_HEADS, MEM_HEAD_DIM).astype(F32)
    s = jnp.einsum('bqhd,bmhd->bhqm', q, kv[:, :, 0]) * (MEM_HEAD_DIM ** -0.5)
    p = jax.nn.softmax(s, axis=-1)
    out = jnp.einsum('bhqm,bmhd->bqhd', p, kv[:, :, 1])
    return out.reshape(bsz, seq, MEM_WIDTH).astype(mq.dtype)


def _moe_ffn(x, router_w, router_bias, w_in, w_out):
    bsz, seq, d = x.shape
    t = x.reshape(bsz * seq, d)
    n_tok = t.shape[0]
    s = jax.nn.sigmoid(t.astype(F32) @ router_w.astype(F32))
    sb = s + router_bias.astype(F32)
    g_score = lax.top_k(sb.reshape(-1, N_GROUPS, EXPERTS_PER_GROUP), 2)[0].sum(-1)
    _, g_sel = lax.top_k(g_score, TOPK_GROUPS)
    g_mask = jnp.any(g_sel[..., None] == jnp.arange(N_GROUPS), axis=1)
    e_mask = jnp.repeat(g_mask, EXPERTS_PER_GROUP, axis=1)
    _, e_sel = lax.top_k(jnp.where(e_mask, sb, -jnp.inf), TOPK_EXPERTS)
    gw = jnp.take_along_axis(s, e_sel, axis=1)
    gw = gw / jnp.sum(gw, axis=-1, keepdims=True)

    n_asg = n_tok * TOPK_EXPERTS
    n_chunks = -(-n_asg // MOE_CHUNK) + N_EXPERTS
    e_flat = e_sel.reshape(-1)
    order = jnp.argsort(e_flat)
    e_sorted = e_flat[order]
    tok_sorted = order // TOPK_EXPERTS
    w_sorted = gw.reshape(-1)[order]
    counts = jnp.bincount(e_flat, length=N_EXPERTS)
    padded = (counts + MOE_CHUNK - 1) // MOE_CHUNK * MOE_CHUNK
    pad_end = jnp.cumsum(padded)
    pad_start = pad_end - padded
    start = jnp.cumsum(counts) - counts
    slot = pad_start[e_sorted] + jnp.arange(n_asg) - start[e_sorted]
    buf = jnp.zeros((n_chunks * MOE_CHUNK, d), t.dtype).at[slot].set(t[tok_sorted])
    chunk_exp = jnp.clip(jnp.searchsorted(pad_end, jnp.arange(n_chunks) * MOE_CHUNK, side='right'),
                         0, N_EXPERTS - 1)
    xb = buf.reshape(n_chunks, MOE_CHUNK, d)
    h = jnp.einsum('ncd,ndf->ncf', xb, w_in[chunk_exp])
    h = jax.nn.silu(h[..., :EXPERT_FF]) * h[..., EXPERT_FF:]
    yb = jnp.einsum('ncf,nfd->ncd', h, w_out[chunk_exp]).reshape(-1, d)
    y_asg = yb[slot].astype(F32) * w_sorted[:, None]
    y = jax.ops.segment_sum(y_asg, tok_sorted, num_segments=n_tok)
    return y.reshape(bsz, seq, d).astype(x.dtype)


def setup_inputs(seed: int = 0) -> dict:
    key = jax.random.key(seed)
    keys = iter(jax.random.split(key, 64))

    def nrm(shape, scale):
        return jax.random.normal(next(keys), shape, F32) * scale

    def gain(shape):
        return 1.0 + nrm(shape, 0.02)

    def mem_wkv():
        w = nrm((D_MODEL, 2 * MEM_WIDTH), D_MODEL ** -0.5)
        return w.at[:, MEM_WIDTH:].multiply(DEEPNORM_BETA)

    def w_o():
        return nrm((OUT_WIDTH, D_MODEL), OUT_WIDTH ** -0.5 * DEEPNORM_BETA)

    def moe_in():
        return nrm((N_EXPERTS, D_MODEL, 2 * EXPERT_FF), D_MODEL ** -0.5)

    def moe_out():
        return nrm((N_EXPERTS, EXPERT_FF, D_MODEL), EXPERT_FF ** -0.5 * DEEPNORM_BETA)

    inp = {}
    inp['x'] = nrm((BATCH, SEQ, D_MODEL), 1.0)
    inp['mem'] = nrm((BATCH, MEM_TOKENS, D_MODEL), 1.0)
    inp['positions'] = (jax.random.randint(next(keys), (BATCH, 1), 0, 4096, dtype=jnp.int32)
                        + jnp.arange(SEQ, dtype=jnp.int32)[None, :])
    inp['router_w'] = nrm((D_MODEL, N_EXPERTS), D_MODEL ** -0.5)
    inp['router_bias'] = nrm((N_EXPERTS,), 0.01)
    inp['l0_w_in'] = nrm((D_MODEL, W_IN_SB), D_MODEL ** -0.5)
    inp['l0_mem_wkv'] = mem_wkv()
    inp['l0_w_o'] = w_o()
    inp['l0_ln_g'] = gain((2, D_MODEL))
    inp['l0_ln_b'] = nrm((2, D_MODEL), 0.02)
    inp['l0_moe_w_in'] = moe_in()
    inp['l0_moe_w_out'] = moe_out()
    inp['l1_w_in'] = nrm((D_MODEL, W_IN_CONV), D_MODEL ** -0.5)
    inp['l1_conv_w'] = nrm((CONV_WIDTH, CONV_CH), CONV_WIDTH ** -0.5)
    inp['l1_conv_b'] = nrm((CONV_CH,), 0.02)
    inp['l1_conv_norm_g'] = gain((CONV_CH,))
    inp['l1_conv_norm_b'] = nrm((CONV_CH,), 0.02)
    inp['l1_mem_wkv'] = mem_wkv()
    inp['l1_w_o'] = w_o()
    inp['l1_ln_g'] = gain((2, D_MODEL))
    inp['l1_ln_b'] = nrm((2, D_MODEL), 0.02)
    inp['l1_moe_w_in'] = moe_in()
    inp['l1_moe_w_out'] = moe_out()
    inp['l2_w_in'] = nrm((D_MODEL, W_IN_DSA), D_MODEL ** -0.5)
    inp['l2_mem_wkv'] = mem_wkv()
    inp['l2_w_o'] = w_o()
    inp['l2_ln_g'] = gain((2, D_MODEL))
    inp['l2_ln_b'] = nrm((2, D_MODEL), 0.02)
    inp['l2_moe_w_in'] = moe_in()
    inp['l2_moe_w_out'] = moe_out()
    inp['l3_w_in'] = nrm((D_MODEL, W_IN_MLA), D_MODEL ** -0.5)
    inp['l3_q_norm_g'] = gain((MLA_Q_RANK,))
    inp['l3_kv_norm_g'] = gain((MLA_KV_RANK,))
    inp['l3_w_q_up'] = nrm((MLA_Q_RANK, MLA_HEADS * (MLA_NOPE + MLA_ROPE)), MLA_Q_RANK ** -0.5)
    inp['l3_w_kv_up'] = nrm((MLA_KV_RANK, MLA_HEADS * (MLA_NOPE + MLA_V)), MLA_KV_RANK ** -0.5)
    inp['l3_mem_wkv'] = mem_wkv()
    inp['l3_w_o'] = w_o()
    inp['l3_ln_g'] = gain((2, D_MODEL))
    inp['l3_ln_b'] = nrm((2, D_MODEL), 0.02)
    inp['l3_moe_w_in'] = moe_in()
    inp['l3_moe_w_out'] = moe_out()
    return inp


def reference(x, mem, positions, router_w, router_bias,
              l0_w_in, l0_mem_wkv, l0_w_o, l0_ln_g, l0_ln_b, l0_moe_w_in, l0_moe_w_out,
              l1_w_in, l1_conv_w, l1_conv_b, l1_conv_norm_g, l1_conv_norm_b, l1_mem_wkv, l1_w_o,
              l1_ln_g, l1_ln_b, l1_moe_w_in, l1_moe_w_out,
              l2_w_in, l2_mem_wkv, l2_w_o, l2_ln_g, l2_ln_b, l2_moe_w_in, l2_moe_w_out,
              l3_w_in, l3_q_norm_g, l3_kv_norm_g, l3_w_q_up, l3_w_kv_up, l3_mem_wkv, l3_w_o,
              l3_ln_g, l3_ln_b, l3_moe_w_in, l3_moe_w_out):
    layers = [
        dict(w_in=l0_w_in, mem_wkv=l0_mem_wkv, w_o=l0_w_o, ln_g=l0_ln_g, ln_b=l0_ln_b,
             moe_w_in=l0_moe_w_in, moe_w_out=l0_moe_w_out),
        dict(w_in=l1_w_in, mem_wkv=l1_mem_wkv, w_o=l1_w_o, ln_g=l1_ln_g, ln_b=l1_ln_b,
             moe_w_in=l1_moe_w_in, moe_w_out=l1_moe_w_out, conv_w=l1_conv_w, conv_b=l1_conv_b,
             conv_norm_g=l1_conv_norm_g, conv_norm_b=l1_conv_norm_b),
        dict(w_in=l2_w_in, mem_wkv=l2_mem_wkv, w_o=l2_w_o, ln_g=l2_ln_g, ln_b=l2_ln_b,
             moe_w_in=l2_moe_w_in, moe_w_out=l2_moe_w_out),
        dict(w_in=l3_w_in, mem_wkv=l3_mem_wkv, w_o=l3_w_o, ln_g=l3_ln_g, ln_b=l3_ln_b,
             moe_w_in=l3_moe_w_in, moe_w_out=l3_moe_w_out, q_norm_g=l3_q_norm_g,
             kv_norm_g=l3_kv_norm_g, w_q_up=l3_w_q_up, w_kv_up=l3_w_kv_up),
    ]
    bsz, seq, _ = x.shape
    for i in range(DEPTH):
        p = layers[i]
        kind = MIXER_ORDER[i % N_MIXERS]
        h = x @ p['w_in']
        if kind == 'stick_breaking':
            q, k, v, mq = _split_cols(h, [MIX_WIDTH, MIX_WIDTH, MIX_WIDTH, MEM_WIDTH])
            shp = (bsz, seq, SB_HEADS, SB_HEAD_DIM)
            mix = _stick_breaking_attention(q.reshape(shp), k.reshape(shp), v.reshape(shp))
            mix = mix.reshape(bsz, seq, MIX_WIDTH)
        elif kind == 'conformer_conv':
            a, g, mq = _split_cols(h, [CONV_CH, CONV_CH, MEM_WIDTH])
            mix = _conformer_conv(a, g, p['conv_w'], p['conv_b'], p['conv_norm_g'], p['conv_norm_b'])
        elif kind == 'dsa_sparse':
            q, k, v, qi, ki, wi, mq = _split_cols(
                h, [MIX_WIDTH, DSA_HEAD_DIM, DSA_HEAD_DIM, IDX_HEADS * IDX_DIM, IDX_DIM, IDX_HEADS, MEM_WIDTH])
            q = _rope(q.reshape(bsz, seq, DSA_HEADS, DSA_HEAD_DIM), positions)
            k = _rope(k[:, :, None, :], positions)[:, :, 0]
            qi = _rope(qi.reshape(bsz, seq, IDX_HEADS, IDX_DIM), positions)
            ki = _rope(ki[:, :, None, :], positions)[:, :, 0]
            mix = _dsa_attention(q, k, v, qi, ki, wi).reshape(bsz, seq, MIX_WIDTH)
        else:
            c_q, c_kv, k_rope, mq = _split_cols(h, [MLA_Q_RANK, MLA_KV_RANK, MLA_ROPE, MEM_WIDTH])
            mix = _mla_attention(c_q, c_kv, k_rope, positions, p['q_norm_g'], p['kv_norm_g'],
                                 p['w_q_up'], p['w_kv_up'])
        mem_out = _memory_attention(mq, mem, p['mem_wkv'])
        y = jnp.concatenate([mix, mem_out], axis=-1) @ p['w_o']
        x = _layer_norm(DEEPNORM_ALPHA * x + y, p['ln_g'][0], p['ln_b'][0])
        f = _moe_ffn(x, router_w, router_bias, p['moe_w_in'], p['moe_w_out'])
        x = _layer_norm(DEEPNORM_ALPHA * x + f, p['ln_g'][1], p['ln_b'][1])
    return x
```

```python
import functools

import jax
import jax.numpy as jnp
from jax import lax
from jax.experimental import pallas as pl
from jax.experimental.pallas import tpu as pltpu

F32 = jnp.float32
BF16 = jnp.bfloat16
I32 = jnp.int32

D_MODEL = 1024
DEPTH = 4
ROPE_THETA = 10000.0
LN_EPS = 1e-5
RMS_EPS = 1e-6
DEEPNORM_ALPHA = (2.0 * DEPTH) ** 0.25
SB_HEADS = 8
CONV_WIDTH = 31
DSA_HEADS = 16
DSA_HEAD_DIM = 64
IDX_HEADS = 8
IDX_DIM = 64
IDX_TOPK_MAX = 256
MLA_HEADS = 8
MLA_Q_RANK = 384
MLA_KV_RANK = 256
MLA_NOPE = 64
MLA_ROPE = 32
MLA_V = 128
MEM_HEADS = 4
MEM_HEAD_DIM = 64
MEM_WIDTH = 256
N_EXPERTS = 16
N_GROUPS = 4
EXPERT_FF = 512

LANES = 128
VMEM_LIMIT_BYTES = 56 * 1024 * 1024
NEG_BIG = -1e30
INT_MIN = -2 ** 31

_NT = (((1,), (1,)), ((), ()))


def _cparams(sem):
    return pltpu.CompilerParams(dimension_semantics=sem, vmem_limit_bytes=VMEM_LIMIT_BYTES)


def _layer_norm(v, g, b):
    mu = jnp.mean(v, axis=-1, keepdims=True)
    d = v - mu
    var = jnp.mean(d * d, axis=-1, keepdims=True)
    return d * lax.rsqrt(var + LN_EPS) * g + b


def _sigmoid(v):
    return 1.0 / (1.0 + jnp.exp(-v))


def _rope_table_kernel(pos_ref, freq_ref, sign_ref, cos_ref, sin_ref):
    ang = pos_ref[...] * freq_ref[...]
    cos_ref[...] = jnp.cos(ang)
    sin_ref[...] = jnp.sin(ang) * sign_ref[...]


def _rope_tables(pos_b, freq_row, sign_row, tm=512):
    t = pos_b.shape[0]
    row = pl.BlockSpec((tm, LANES), lambda i: (i, 0))
    one = pl.BlockSpec((1, LANES), lambda i: (0, 0))
    return pl.pallas_call(
        _rope_table_kernel,
        grid=(t // tm,),
        in_specs=[row, one, one],
        out_specs=[row, row],
        out_shape=[jax.ShapeDtypeStruct((t, LANES), F32)] * 2,
        compiler_params=_cparams(("parallel",)),
        name="rope_tables",
    )(pos_b, freq_row, sign_row)


def _rope128(v, cos, sin, half):
    lane = lax.broadcasted_iota(I32, v.shape, 1)
    first = (lane & (2 * half - 1)) < half
    fwd = pltpu.roll(v, LANES - half, 1)
    bwd = pltpu.roll(v, half, 1)
    return v * cos + jnp.where(first, fwd, bwd) * sin


def _proj_kernel(*refs, groups, has_rope):
    x_ref, w_ref = refs[0], refs[1]
    nin = 4 if has_rope else 2
    outs = refs[nin:]
    x = x_ref[...].astype(BF16)
    for (start, width, out_width, rope_half), o_ref in zip(groups, outs):
        acc = jnp.dot(x, w_ref[:, start:start + width], preferred_element_type=F32)
        if rope_half:
            cos = refs[2][...]
            sin = refs[3][...]
            for c in range(width // LANES):
                r = _rope128(acc[:, c * LANES:(c + 1) * LANES], cos, sin, rope_half)
                lo = c * LANES
                hi = min(lo + LANES, out_width)
                o_ref[:, lo:hi] = r[:, :hi - lo].astype(o_ref.dtype)
        else:
            o_ref[...] = acc[:, :out_width].astype(o_ref.dtype)


def _proj(x, w, groups, out_dtypes, rope=None, tm=512, name="proj"):
    m, k = x.shape
    n = w.shape[1]
    tm = min(tm, m)
    in_specs = [pl.BlockSpec((tm, k), lambda i: (i, 0)), pl.BlockSpec((k, n), lambda i: (0, 0))]
    args = [x, w]
    if rope is not None:
        in_specs += [pl.BlockSpec((tm, LANES), lambda i: (i, 0))] * 2
        args += list(rope)
    out_specs = [pl.BlockSpec((tm, g[2]), lambda i: (i, 0)) for g in groups]
    out_shape = [jax.ShapeDtypeStruct((m, g[2]), dt) for g, dt in zip(groups, out_dtypes)]
    return pl.pallas_call(
        functools.partial(_proj_kernel, groups=tuple(groups), has_rope=rope is not None),
        grid=(m // tm,),
        in_specs=in_specs,
        out_specs=out_specs,
        out_shape=out_shape,
        compiler_params=_cparams(("parallel",)),
        name=name,
    )(*args)


def _sb_kernel(q_ref, k_ref, v_ref, o_ref, *, tq, scale):
    i = pl.program_id(2)
    q = q_ref[...]
    row = lax.broadcasted_iota(I32, (tq, tq), 0)
    col = lax.broadcasted_iota(I32, (tq, tq), 1)
    suffix = (row >= col).astype(BF16)
    strict = col < row

    def block(j, carry, diag):
        tail, acc = carry
        start = pl.multiple_of(j * tq, tq)
        kb = k_ref[pl.ds(start, tq), :]
        vb = v_ref[pl.ds(start, tq), :]
        z = lax.dot_general(q, kb, _NT, preferred_element_type=F32) * scale
        sp = jnp.maximum(z, 0.0) + jnp.log(1.0 + jnp.exp(-jnp.abs(z)))
        if diag:
            sp = jnp.where(strict, sp, 0.0)
        hi = sp.astype(BF16)
        lo = (sp - hi.astype(F32)).astype(BF16)
        cum = (jnp.dot(hi, suffix, preferred_element_type=F32)
               + jnp.dot(lo, suffix, preferred_element_type=F32))
        a = jnp.exp(z - cum - tail)
        if diag:
            a = jnp.where(strict, a, 0.0)
        acc = acc + jnp.dot(a.astype(BF16), vb, preferred_element_type=F32)
        tail = tail + jnp.sum(sp, axis=-1, keepdims=True)
        return tail, acc

    carry = (jnp.zeros((tq, 1), F32), jnp.zeros((tq, LANES), F32))
    carry = block(i, carry, True)
    _, acc = lax.fori_loop(0, i, lambda jj, c: block(i - 1 - jj, c, False), carry)
    o_ref[...] = acc.astype(o_ref.dtype)


def _sb_attention(h, bsz, seq, tq=256):
    tq = min(tq, seq)
    nq = seq // tq
    t = bsz * seq
    return pl.pallas_call(
        functools.partial(_sb_kernel, tq=tq, scale=LANES ** -0.5),
        grid=(bsz, SB_HEADS, nq),
        in_specs=[
            pl.BlockSpec((tq, LANES), lambda b, hh, i: (b * nq + i, hh)),
            pl.BlockSpec((seq, LANES), lambda b, hh, i: (b, SB_HEADS + hh)),
            pl.BlockSpec((seq, LANES), lambda b, hh, i: (b, 2 * SB_HEADS + hh)),
        ],
        out_specs=pl.BlockSpec((tq, LANES), lambda b, hh, i: (b * nq + i, hh)),
        out_shape=jax.ShapeDtypeStruct((t, SB_HEADS * LANES), BF16),
        compiler_params=_cparams(("parallel", "parallel", "arbitrary")),
        name="sb_attention",
    )(h, h, h)


def _conv_kernel(a_ref, g_ref, ap_ref, gp_ref, w_ref, cb_ref, ng_ref, nb_ref, o_ref, hext_ref, y_ref,
                 *, tm, halo, tiles_per_seq):
    i = pl.program_id(0)
    hext_ref[halo:halo + tm, :] = a_ref[...] * _sigmoid(g_ref[...])
    prev = ap_ref[...] * _sigmoid(gp_ref[...])
    first = (i % tiles_per_seq) == 0
    hext_ref[0:halo, :] = jnp.where(first, 0.0, prev)
    off = halo - (CONV_WIDTH - 1)
    for c in range(D_MODEL // LANES):
        cs = slice(c * LANES, (c + 1) * LANES)
        acc = jnp.zeros((tm, LANES), F32)
        for j in range(CONV_WIDTH):
            acc = acc + w_ref[j:j + 1, cs] * hext_ref[off + j:off + j + tm, cs]
        y_ref[:, cs] = acc
    y = y_ref[...] + cb_ref[...]
    y = _layer_norm(y, ng_ref[...], nb_ref[...])
    o_ref[...] = (y * _sigmoid(y)).astype(o_ref.dtype)


def _conformer_conv(a, g, conv_w, conv_b, norm_g, norm_b, seq, tm=256, halo=32):
    t = a.shape[0]
    tm = min(tm, seq)
    r = tm // halo
    cur = pl.BlockSpec((tm, D_MODEL), lambda i: (i, 0))
    prv = pl.BlockSpec((halo, D_MODEL), lambda i: (jnp.maximum(i * r - 1, 0), 0))
    vec = pl.BlockSpec((1, D_MODEL), lambda i: (0, 0))
    return pl.pallas_call(
        functools.partial(_conv_kernel, tm=tm, halo=halo, tiles_per_seq=seq // tm),
        grid=(t // tm,),
        in_specs=[cur, cur, prv, prv, pl.BlockSpec((CONV_WIDTH, D_MODEL), lambda i: (0, 0)), vec, vec, vec],
        out_specs=cur,
        out_shape=jax.ShapeDtypeStruct((t, D_MODEL), BF16),
        scratch_shapes=[pltpu.VMEM((tm + halo, D_MODEL), F32), pltpu.VMEM((tm, D_MODEL), F32)],
        compiler_params=_cparams(("parallel",)),
        name="conformer_conv",
    )(a, g, a, g, conv_w, conv_b.reshape(1, -1), norm_g.reshape(1, -1), norm_b.reshape(1, -1))


def _dsa_kernel(q_ref, qi_ref, wi_ref, k_ref, v_ref, ki_ref, o_ref,
                qs_ref, qis_ref, keys_ref, m_ref, l_ref, acc_ref, *, tq, tkb, topk, scale):
    i = pl.program_id(1)
    nkb = (i * tq + tq + tkb - 1) // tkb
    for h in range(DSA_HEADS):
        qs_ref[h * tq:(h + 1) * tq, :] = q_ref[:, h * DSA_HEAD_DIM:(h + 1) * DSA_HEAD_DIM]
    for h in range(IDX_HEADS):
        qis_ref[h * tq:(h + 1) * tq, :] = qi_ref[:, h * IDX_DIM:(h + 1) * IDX_DIM]
    wi = wi_ref[...] * (IDX_HEADS ** -0.5 * IDX_DIM ** -0.5)
    wcols = [wi[:, h:h + 1] for h in range(IDX_HEADS)]
    q_pos = i * tq + lax.broadcasted_iota(I32, (tq, 1), 0)
    col0 = lax.broadcasted_iota(I32, (tq, tkb), 1)

    def scores(jb, _):
        start = pl.multiple_of(jb * tkb, tkb)
        kib = ki_ref[pl.ds(start, tkb), :]
        dots = lax.dot_general(qis_ref[...], kib, _NT, preferred_element_type=F32)
        d3 = dots.reshape(IDX_HEADS, tq, tkb)
        sc = wcols[0] * jnp.maximum(d3[0], 0.0)
        for h in range(1, IDX_HEADS):
            sc = sc + wcols[h] * jnp.maximum(d3[h], 0.0)
        sc = sc + 0.0
        bits = lax.bitcast_convert_type(sc, I32)
        key = bits ^ ((bits >> 31) & 0x7FFFFFFF)
        key = jnp.where(col0 + start <= q_pos, key, INT_MIN)
        keys_ref[:, pl.ds(start, tkb)] = key
        return 0

    lax.fori_loop(0, nkb, scores, 0)

    def count_ge(cand):
        def body(c, acc):
            start = pl.multiple_of(c * tkb, tkb)
            ind = jnp.where(keys_ref[:, pl.ds(start, tkb)] >= cand, 1.0, 0.0)
            for s in range(tkb // LANES):
                acc = acc + ind[:, s * LANES:(s + 1) * LANES]
            return acc
        acc = lax.fori_loop(0, nkb, body, jnp.zeros((tq, LANES), F32))
        return jnp.sum(acc, axis=-1, keepdims=True)

    kf = float(topk)
    tau = jnp.where(count_ge(jnp.zeros((tq, 1), I32)) >= kf, 0, INT_MIN).astype(I32)

    def bisect(it, tau):
        cand = tau + jnp.left_shift(jnp.int32(1), 30 - it)
        return jnp.where(count_ge(cand) >= kf, cand, tau)

    tau = lax.fori_loop(0, 31, bisect, tau)
    tau = jnp.maximum(tau, INT_MIN + 1)

    m_ref[...] = jnp.full(m_ref.shape, NEG_BIG, F32)
    l_ref[...] = jnp.zeros(l_ref.shape, F32)
    acc_ref[...] = jnp.zeros(acc_ref.shape, F32)

    def attend(jb, _):
        start = pl.multiple_of(jb * tkb, tkb)
        kb = k_ref[pl.ds(start, tkb), :]
        vb = v_ref[pl.ds(start, tkb), :]
        s = lax.dot_general(qs_ref[...], kb, _NT, preferred_element_type=F32) * scale
        sel = keys_ref[:, pl.ds(start, tkb)] >= tau
        s3 = jnp.where(sel[None], s.reshape(DSA_HEADS, tq, tkb), NEG_BIG)
        m_old = m_ref[...]
        m_new = jnp.maximum(m_old, jnp.max(s3, axis=-1, keepdims=True))
        alpha = jnp.exp(m_old - m_new)
        p = jnp.exp(s3 - m_new)
        l_ref[...] = alpha * l_ref[...] + jnp.sum(p, axis=-1, keepdims=True)
        pv = jnp.dot(p.reshape(DSA_HEADS * tq, tkb).astype(BF16), vb, preferred_element_type=F32)
        acc_ref[...] = alpha * acc_ref[...] + pv.reshape(DSA_HEADS, tq, DSA_HEAD_DIM)
        m_ref[...] = m_new
        return 0

    lax.fori_loop(0, nkb, attend, 0)
    out = acc_ref[...] / l_ref[...]
    for h in range(DSA_HEADS):
        o_ref[:, h * DSA_HEAD_DIM:(h + 1) * DSA_HEAD_DIM] = out[h].astype(o_ref.dtype)


def _dsa_attention(q, qi, wi, k, v, ki, bsz, seq, tq=128, tkb=256):
    t = bsz * seq
    tq = min(tq, seq)
    tkb = min(tkb, seq)
    nq = seq // tq
    topk = min(IDX_TOPK_MAX, seq // 4)
    qrow = lambda width: pl.BlockSpec((tq, width), lambda b, i: (b * nq + i, 0))
    kv = pl.BlockSpec((seq, DSA_HEAD_DIM), lambda b, i: (b, 0))
    return pl.pallas_call(
        functools.partial(_dsa_kernel, tq=tq, tkb=tkb, topk=topk, scale=DSA_HEAD_DIM ** -0.5),
        grid=(bsz, nq),
        in_specs=[qrow(DSA_HEADS * DSA_HEAD_DIM), qrow(IDX_HEADS * IDX_DIM), qrow(LANES), kv, kv, kv],
        out_specs=qrow(DSA_HEADS * DSA_HEAD_DIM),
        out_shape=jax.ShapeDtypeStruct((t, DSA_HEADS * DSA_HEAD_DIM), BF16),
        scratch_shapes=[
            pltpu.VMEM((DSA_HEADS * tq, DSA_HEAD_DIM), BF16),
            pltpu.VMEM((IDX_HEADS * tq, IDX_DIM), BF16),
            pltpu.VMEM((tq, seq), I32),
            pltpu.VMEM((DSA_HEADS, tq, 1), F32),
            pltpu.VMEM((DSA_HEADS, tq, 1), F32),
            pltpu.VMEM((DSA_HEADS, tq, DSA_HEAD_DIM), F32),
        ],
        compiler_params=_cparams(("parallel", "arbitrary")),
        name="dsa_attention",
    )(q, qi, wi, k, v, ki)


def _mla_prep_kernel(cq_ref, ckv_ref, kr_ref, cos_ref, sin_ref, qg_ref, kvg_ref, wq_ref, wk_ref, wv_ref,
                     q_ref, k_ref, v_ref):
    def rms(c, g):
        return c * lax.rsqrt(jnp.mean(c * c, axis=-1, keepdims=True) + RMS_EPS) * g

    cos = cos_ref[...]
    sin = sin_ref[...]
    qn = rms(cq_ref[...], qg_ref[...]).astype(BF16)
    kvn = rms(ckv_ref[...], kvg_ref[...]).astype(BF16)
    kr = kr_ref[...]
    for h in range(MLA_HEADS):
        hs = slice(h * LANES, (h + 1) * LANES)
        qh = jnp.dot(qn, wq_ref[:, hs], preferred_element_type=F32)
        q_ref[:, hs] = _rope128(qh, cos, sin, MLA_ROPE // 2).astype(q_ref.dtype)
        kh = jnp.dot(kvn, wk_ref[:, hs], preferred_element_type=F32)
        k_ref[:, hs] = (kh + kr).astype(k_ref.dtype)
        v_ref[:, hs] = jnp.dot(kvn, wv_ref[:, hs], preferred_element_type=F32).astype(v_ref.dtype)


def _mla_prep(c_q, c_kv, kr, cos, sin, q_norm_g, kv_norm_g, wq, wk, wv, tm=512):
    t = c_q.shape[0]
    tm = min(tm, t)
    row = lambda width: pl.BlockSpec((tm, width), lambda i: (i, 0))
    full = lambda a: pl.BlockSpec(a.shape, lambda i: (0, 0))
    qg = q_norm_g.reshape(1, -1)
    kvg = kv_norm_g.reshape(1, -1)
    width = MLA_HEADS * LANES
    return pl.pallas_call(
        _mla_prep_kernel,
        grid=(t // tm,),
        in_specs=[row(MLA_Q_RANK), row(MLA_KV_RANK), row(LANES), row(LANES), row(LANES),
                  full(qg), full(kvg), full(wq), full(wk), full(wv)],
        out_specs=[row(width)] * 3,
        out_shape=[jax.ShapeDtypeStruct((t, width), BF16)] * 3,
        compiler_params=_cparams(("parallel",)),
        name="mla_prep",
    )(c_q, c_kv, kr, cos, sin, qg, kvg, wq, wk, wv)


def _flash_kernel(q_ref, k_ref, v_ref, o_ref, *, tq, scale):
    i = pl.program_id(2)
    q = q_ref[...]
    row = lax.broadcasted_iota(I32, (tq, tq), 0)
    col = lax.broadcasted_iota(I32, (tq, tq), 1)
    causal = col <= row

    def block(j, carry, diag):
        m, l, acc = carry
        start = pl.multiple_of(j * tq, tq)
        kb = k_ref[pl.ds(start, tq), :]
        vb = v_ref[pl.ds(start, tq), :]
        s = lax.dot_general(q, kb, _NT, preferred_element_type=F32) * scale
        if diag:
            s = jnp.where(causal, s, NEG_BIG)
        m_new = jnp.maximum(m, jnp.max(s, axis=-1, keepdims=True))
        alpha = jnp.exp(m - m_new)
        p = jnp.exp(s - m_new)
        l = alpha * l + jnp.sum(p, axis=-1, keepdims=True)
        acc = alpha * acc + jnp.dot(p.astype(BF16), vb, preferred_element_type=F32)
        return m_new, l, acc

    carry = (jnp.full((tq, 1), NEG_BIG, F32), jnp.zeros((tq, 1), F32), jnp.zeros((tq, LANES), F32))
    carry = block(i, carry, True)
    _, l, acc = lax.fori_loop(0, i, lambda jj, c: block(i - 1 - jj, c, False), carry)
    o_ref[...] = (acc / l).astype(o_ref.dtype)


def _causal_attention(q, k, v, bsz, seq, heads, scale, tq=256):
    tq = min(tq, seq)
    nq = seq // tq
    t = bsz * seq
    kv = pl.BlockSpec((seq, LANES), lambda b, hh, i: (b, hh))
    qo = pl.BlockSpec((tq, LANES), lambda b, hh, i: (b * nq + i, hh))
    return pl.pallas_call(
        functools.partial(_flash_kernel, tq=tq, scale=scale),
        grid=(bsz, heads, nq),
        in_specs=[qo, kv, kv],
        out_specs=qo,
        out_shape=jax.ShapeDtypeStruct((t, heads * LANES), BF16),
        compiler_params=_cparams(("parallel", "parallel", "arbitrary")),
        name="mla_attention",
    )(q, k, v)


def _route(sb, s):
    per = N_EXPERTS // N_GROUPS
    gscore = []
    for g in range(N_GROUPS):
        a, b, c, d = sb[per * g:per * g + per]
        hi1, lo1 = jnp.maximum(a, b), jnp.minimum(a, b)
        hi2, lo2 = jnp.maximum(c, d), jnp.minimum(c, d)
        top1 = jnp.maximum(hi1, hi2)
        top2 = jnp.maximum(jnp.minimum(hi1, hi2), jnp.maximum(lo1, lo2))
        gscore.append(top1 + top2)
    best, gidx = gscore[0], jnp.zeros(gscore[0].shape, I32)
    for g in range(1, N_GROUPS):
        upd = gscore[g] > best
        best = jnp.where(upd, gscore[g], best)
        gidx = jnp.where(upd, g, gidx)

    def pick(rows, j):
        out = rows[j]
        for g in range(1, N_GROUPS):
            out = jnp.where(gidx == g, rows[per * g + j], out)
        return out

    vals = [pick(sb, j) for j in range(per)]
    raws = [pick(s, j) for j in range(per)]
    b1, i1, s1 = vals[0], jnp.zeros(gidx.shape, I32), raws[0]
    for j in range(1, per):
        upd = vals[j] > b1
        b1 = jnp.where(upd, vals[j], b1)
        i1 = jnp.where(upd, j, i1)
        s1 = jnp.where(upd, raws[j], s1)
    b2, i2, s2 = jnp.zeros_like(b1), jnp.full(gidx.shape, -1, I32), jnp.zeros_like(b1)
    for j in range(per):
        upd = (i1 != j) & ((i2 < 0) | (vals[j] > b2))
        b2 = jnp.where(upd, vals[j], b2)
        i2 = jnp.where(upd, j, i2)
        s2 = jnp.where(upd, raws[j], s2)
    denom = s1 + s2
    w1 = s1 / denom
    w2 = s2 / denom
    gates = []
    for e in range(N_EXPERTS):
        in_group = gidx == (e // per)
        gates.append(jnp.where(in_group & (i1 == e % per), w1,
                               jnp.where(in_group & (i2 == e % per), w2, 0.0)))
    return gates


def _post_kernel(x_ref, mix_ref, mq_ref, kvm_ref, wo_ref, g_ref, b_ref, rw_ref, rb_ref, x1_ref, gate_ref):
    mq = mq_ref[...]
    kvm = kvm_ref[0]
    mem = []
    for h in range(MEM_HEADS):
        hs = slice(h * MEM_HEAD_DIM, (h + 1) * MEM_HEAD_DIM)
        vs = slice(MEM_WIDTH + h * MEM_HEAD_DIM, MEM_WIDTH + (h + 1) * MEM_HEAD_DIM)
        s = lax.dot_general(mq[:, hs], kvm[:, hs], _NT, preferred_element_type=F32) * (MEM_HEAD_DIM ** -0.5)
        p = jnp.exp(s - jnp.max(s, axis=-1, keepdims=True))
        p = p / jnp.sum(p, axis=-1, keepdims=True)
        mem.append(jnp.dot(p.astype(BF16), kvm[:, vs], preferred_element_type=F32))
    y = jnp.dot(mix_ref[...], wo_ref[0:D_MODEL, :], preferred_element_type=F32)
    for h in range(MEM_HEADS):
        r0 = D_MODEL + h * MEM_HEAD_DIM
        y = y + jnp.dot(mem[h].astype(BF16), wo_ref[r0:r0 + MEM_HEAD_DIM, :], preferred_element_type=F32)
    x1 = _layer_norm(DEEPNORM_ALPHA * x_ref[...] + y, g_ref[...], b_ref[...])
    x1_ref[...] = x1
    logits = lax.dot_general(rw_ref[...], x1.astype(BF16), _NT, preferred_element_type=F32)
    s_all = _sigmoid(logits)
    sb_all = s_all + rb_ref[...]
    gates = _route([sb_all[e:e + 1, :] for e in range(N_EXPERTS)], [s_all[e:e + 1, :] for e in range(N_EXPERTS)])
    for e in range(N_EXPERTS):
        gate_ref[e:e + 1, :] = gates[e]


def _post_mixer(x, mix, mq, kvm, w_o, ln_g, ln_b, rw_t, rbias, seq, tm=256):
    t = x.shape[0]
    tm = min(tm, seq)
    per_seq = seq // tm
    row = lambda width: pl.BlockSpec((tm, width), lambda i: (i, 0))
    full = lambda a: pl.BlockSpec(a.shape, lambda i: (0,) * a.ndim)
    g = ln_g.reshape(1, -1)
    b = ln_b.reshape(1, -1)
    return pl.pallas_call(
        _post_kernel,
        grid=(t // tm,),
        in_specs=[row(D_MODEL), row(D_MODEL), row(MEM_WIDTH),
                  pl.BlockSpec((1,) + kvm.shape[1:], lambda i: (i // per_seq, 0, 0)),
                  full(w_o), full(g), full(b), full(rw_t), full(rbias)],
        out_specs=[row(D_MODEL), pl.BlockSpec((N_EXPERTS, tm), lambda i: (0, i))],
        out_shape=[jax.ShapeDtypeStruct((t, D_MODEL), F32), jax.ShapeDtypeStruct((N_EXPERTS, t), F32)],
        compiler_params=_cparams(("parallel",)),
        name="post_mixer",
    )(x, mix, mq, kvm, w_o, g, b, rw_t, rbias)


def _moe_kernel(x_ref, gate_ref, win_ref, wout_ref, g_ref, b_ref, o_ref, xb_ref, y_ref):
    e = pl.program_id(1)

    @pl.when(e == 0)
    def _():
        xb_ref[...] = x_ref[...].astype(BF16)
        y_ref[...] = jnp.zeros(y_ref.shape, F32)

    h = jnp.dot(xb_ref[...], win_ref[0], preferred_element_type=F32)
    gate_in = h[:, :EXPERT_FF]
    act = gate_in * _sigmoid(gate_in) * h[:, EXPERT_FF:]
    yb = jnp.dot(act.astype(BF16), wout_ref[0], preferred_element_type=F32)
    gates = gate_ref[...]
    lane = lax.broadcasted_iota(I32, gates.shape, 1)
    gcol = jnp.sum(jnp.where(lane == e, gates, 0.0), axis=-1, keepdims=True)
    y_ref[...] += yb * gcol

    @pl.when(e == N_EXPERTS - 1)
    def _():
        o_ref[...] = _layer_norm(DEEPNORM_ALPHA * x_ref[...] + y_ref[...], g_ref[...], b_ref[...])


def _moe(x1, gates, w_in, w_out, ln_g, ln_b, tm=512):
    t = x1.shape[0]
    tm = min(tm, t)
    g = ln_g.reshape(1, -1)
    b = ln_b.reshape(1, -1)
    vec = pl.BlockSpec((1, D_MODEL), lambda i, e: (0, 0))
    return pl.pallas_call(
        _moe_kernel,
        grid=(t // tm, N_EXPERTS),
        in_specs=[pl.BlockSpec((tm, D_MODEL), lambda i, e: (i, 0)),
                  pl.BlockSpec((tm, LANES), lambda i, e: (i, 0)),
                  pl.BlockSpec((1, D_MODEL, 2 * EXPERT_FF), lambda i, e: (e, 0, 0)),
                  pl.BlockSpec((1, EXPERT_FF, D_MODEL), lambda i, e: (e, 0, 0)),
                  vec, vec],
        out_specs=pl.BlockSpec((tm, D_MODEL), lambda i, e: (i, 0)),
        out_shape=jax.ShapeDtypeStruct((t, D_MODEL), F32),
        scratch_shapes=[pltpu.VMEM((tm, D_MODEL), BF16), pltpu.VMEM((tm, D_MODEL), F32)],
        compiler_params=_cparams(("parallel", "arbitrary")),
        name="moe_ffn",
    )(x1, gates, w_in, w_out, g, b)


def _pad_cols(w, width, offset=0):
    out = jnp.zeros((w.shape[0], width), w.dtype)
    return out.at[:, offset:offset + w.shape[1]].set(w)


def _rope_rows(half, lane_lo, lane_hi):
    inv_freq = ROPE_THETA ** (-jnp.arange(half, dtype=F32) / half)
    lane = jnp.arange(LANES)
    active = (lane >= lane_lo) & (lane < lane_hi)
    pos_in = (lane - lane_lo) % (2 * half)
    freq = jnp.where(active, inv_freq[pos_in % half], 0.0).astype(F32)
    sign = jnp.where(active, jnp.where(pos_in < half, -1.0, 1.0), 0.0).astype(F32)
    return freq.reshape(1, LANES), sign.reshape(1, LANES)


def kernel(x, mem, positions, router_w, router_bias, l0_w_in, l0_mem_wkv, l0_w_o, l0_ln_g, l0_ln_b, l0_moe_w_in, l0_moe_w_out, l1_w_in, l1_conv_w, l1_conv_b, l1_conv_norm_g, l1_conv_norm_b, l1_mem_wkv, l1_w_o, l1_ln_g, l1_ln_b, l1_moe_w_in, l1_moe_w_out, l2_w_in, l2_mem_wkv, l2_w_o, l2_ln_g, l2_ln_b, l2_moe_w_in, l2_moe_w_out, l3_w_in, l3_q_norm_g, l3_kv_norm_g, l3_w_q_up, l3_w_kv_up, l3_mem_wkv, l3_w_o, l3_ln_g, l3_ln_b, l3_moe_w_in, l3_moe_w_out):
    bsz, seq, d = x.shape
    t = bsz * seq
    n_mem = mem.shape[1]
    xt = x.reshape(t, d)
    mem2 = mem.reshape(bsz * n_mem, d)
    pos_b = jnp.broadcast_to(positions.astype(F32).reshape(t, 1), (t, LANES))
    rw_t = router_w.T.astype(BF16)
    rbias = router_bias.astype(F32).reshape(N_EXPERTS, 1)

    def finish_layer(xt, mix, mq, mem_wkv, w_o, ln_g, ln_b, moe_w_in, moe_w_out):
        (kvm,) = _proj(mem2, mem_wkv.astype(BF16), [(0, 2 * MEM_WIDTH, 2 * MEM_WIDTH, 0)], [BF16],
                       tm=n_mem, name="mem_kv")
        kvm = kvm.reshape(bsz, n_mem, 2 * MEM_WIDTH)
        x1, gates_t = _post_mixer(xt, mix, mq, kvm, w_o.astype(BF16), ln_g[0], ln_b[0], rw_t, rbias, seq)
        gates = _pad_cols(gates_t.T, LANES)
        return _moe(x1, gates, moe_w_in.astype(BF16), moe_w_out.astype(BF16), ln_g[1], ln_b[1])

    w0 = l0_w_in.astype(BF16)
    n0 = w0.shape[1]
    (h0,) = _proj(xt, w0, [(0, n0, n0, 0)], [BF16], name="proj0")
    mix = _sb_attention(h0, bsz, seq)
    mq = h0[:, 3 * D_MODEL:]
    xt = finish_layer(xt, mix, mq, l0_mem_wkv, l0_w_o, l0_ln_g, l0_ln_b, l0_moe_w_in, l0_moe_w_out)

    w1 = l1_w_in.astype(BF16)
    a, g, mq = _proj(xt, w1, [(0, D_MODEL, D_MODEL, 0), (D_MODEL, D_MODEL, D_MODEL, 0),
                              (2 * D_MODEL, MEM_WIDTH, MEM_WIDTH, 0)], [F32, F32, BF16], name="proj1")
    mix = _conformer_conv(a, g, l1_conv_w, l1_conv_b, l1_conv_norm_g, l1_conv_norm_b, seq)
    xt = finish_layer(xt, mix, mq, l1_mem_wkv, l1_w_o, l1_ln_g, l1_ln_b, l1_moe_w_in, l1_moe_w_out)

    w2 = l2_w_in
    c = 0
    parts = []
    for width in (DSA_HEADS * DSA_HEAD_DIM, DSA_HEAD_DIM, DSA_HEAD_DIM, IDX_HEADS * IDX_DIM, IDX_DIM, IDX_HEADS,
                  MEM_WIDTH):
        parts.append(w2[:, c:c + width])
        c += width
    wq, wk, wv, wqi, wki, wwi, wmq = parts
    w2p = jnp.concatenate([wq, _pad_cols(wk, LANES), _pad_cols(wv, LANES), wqi, _pad_cols(wki, LANES),
                           _pad_cols(wwi, LANES), wmq], axis=1).astype(BF16)
    freq, sign = _rope_rows(DSA_HEAD_DIM // 2, 0, LANES)
    cos_a, sin_a = _rope_tables(pos_b, freq, sign)
    rh = DSA_HEAD_DIM // 2
    groups = [(0, 1024, 1024, rh), (1024, LANES, DSA_HEAD_DIM, rh), (1152, LANES, DSA_HEAD_DIM, 0),
              (1280, 512, 512, rh), (1792, LANES, IDX_DIM, rh), (1920, LANES, LANES, 0),
              (2048, MEM_WIDTH, MEM_WIDTH, 0)]
    q, k, v, qi, ki, wi, mq = _proj(xt, w2p, groups, [BF16, BF16, BF16, BF16, BF16, F32, BF16],
                                    rope=(cos_a, sin_a), name="proj2")
    mix = _dsa_attention(q, qi, wi, k, v, ki, bsz, seq)
    xt = finish_layer(xt, mix, mq, l2_mem_wkv, l2_w_o, l2_ln_g, l2_ln_b, l2_moe_w_in, l2_moe_w_out)

    w3 = l3_w_in
    rope_lo = MLA_NOPE
    w_cq = w3[:, :MLA_Q_RANK]
    w_ckv = w3[:, MLA_Q_RANK:MLA_Q_RANK + MLA_KV_RANK]
    w_kr = w3[:, MLA_Q_RANK + MLA_KV_RANK:MLA_Q_RANK + MLA_KV_RANK + MLA_ROPE]
    w_mq = w3[:, MLA_Q_RANK + MLA_KV_RANK + MLA_ROPE:]
    w3p = jnp.concatenate([w_cq, w_ckv, _pad_cols(w_kr, LANES, rope_lo), w_mq], axis=1).astype(BF16)
    freq, sign = _rope_rows(MLA_ROPE // 2, rope_lo, rope_lo + MLA_ROPE)
    cos_b, sin_b = _rope_tables(pos_b, freq, sign)
    groups = [(0, MLA_Q_RANK, MLA_Q_RANK, 0), (MLA_Q_RANK, MLA_KV_RANK, MLA_KV_RANK, 0),
              (MLA_Q_RANK + MLA_KV_RANK, LANES, LANES, MLA_ROPE // 2),
              (MLA_Q_RANK + MLA_KV_RANK + LANES, MEM_WIDTH, MEM_WIDTH, 0)]
    c_q, c_kv, kr, mq = _proj(xt, w3p, groups, [F32, F32, F32, BF16], rope=(cos_b, sin_b), name="proj3")
    qk_dim = MLA_NOPE + MLA_ROPE
    wq_heads = l3_w_q_up.reshape(MLA_Q_RANK, MLA_HEADS, qk_dim)
    wq_p = jnp.zeros((MLA_Q_RANK, MLA_HEADS, LANES), F32).at[:, :, :qk_dim].set(wq_heads)
    wkv_heads = l3_w_kv_up.reshape(MLA_KV_RANK, MLA_HEADS, MLA_NOPE + MLA_V)
    wk_p = jnp.zeros((MLA_KV_RANK, MLA_HEADS, LANES), F32).at[:, :, :MLA_NOPE].set(wkv_heads[:, :, :MLA_NOPE])
    wv_p = wkv_heads[:, :, MLA_NOPE:]
    flat = lambda w: w.reshape(w.shape[0], MLA_HEADS * LANES).astype(BF16)
    qh, kh, vh = _mla_prep(c_q, c_kv, kr, cos_b, sin_b, l3_q_norm_g, l3_kv_norm_g, flat(wq_p), flat(wk_p),
                           flat(wv_p))
    mix = _causal_attention(qh, kh, vh, bsz, seq, MLA_HEADS, qk_dim ** -0.5)
    xt = finish_layer(xt, mix, mq, l3_mem_wkv, l3_w_o, l3_ln_g, l3_ln_b, l3_moe_w_in, l3_moe_w_out)
    return xt.reshape(bsz, seq, d)
```

```python
import functools

import jax
import jax.numpy as jnp
from jax import lax
from jax.experimental import pallas as pl
from jax.experimental.pallas import tpu as pltpu

F32 = jnp.float32
BF16 = jnp.bfloat16
I32 = jnp.int32

D_MODEL = 1024
DEPTH = 4
ROPE_THETA = 10000.0
LN_EPS = 1e-5
RMS_EPS = 1e-6
DEEPNORM_ALPHA = (2.0 * DEPTH) ** 0.25
SB_HEADS = 8
CONV_WIDTH = 31
DSA_HEADS = 16
DSA_HEAD_DIM = 64
IDX_HEADS = 8
IDX_DIM = 64
IDX_TOPK_MAX = 256
MLA_HEADS = 8
MLA_Q_RANK = 384
MLA_KV_RANK = 256
MLA_NOPE = 64
MLA_ROPE = 32
MLA_V = 128
MEM_HEADS = 4
MEM_HEAD_DIM = 64
MEM_WIDTH = 256
N_EXPERTS = 16
N_GROUPS = 4
EXPERT_FF = 512

LANES = 128
VMEM_LIMIT_BYTES = 56 * 1024 * 1024
NEG_BIG = -1e30
INT_MIN = -2 ** 31

_NT = (((1,), (1,)), ((), ()))


def _cparams(sem):
    return pltpu.CompilerParams(dimension_semantics=sem, vmem_limit_bytes=VMEM_LIMIT_BYTES)


def _layer_norm(v, g, b):
    mu = jnp.mean(v, axis=-1, keepdims=True)
    d = v - mu
    var = jnp.mean(d * d, axis=-1, keepdims=True)
    return d * lax.rsqrt(var + LN_EPS) * g + b


def _sigmoid(v):
    return 1.0 / (1.0 + jnp.exp(-v))


def _rope_table_kernel(pos_ref, freq_ref, sign_ref, cos_ref, sin_ref):
    ang = pos_ref[...] * freq_ref[...]
    cos_ref[...] = jnp.cos(ang)
    sin_ref[...] = jnp.sin(ang) * sign_ref[...]


def _rope_tables(pos_b, freq_row, sign_row, tm=512):
    t = pos_b.shape[0]
    row = pl.BlockSpec((tm, LANES), lambda i: (i, 0))
    one = pl.BlockSpec((1, LANES), lambda i: (0, 0))
    return pl.pallas_call(
        _rope_table_kernel,
        grid=(t // tm,),
        in_specs=[row, one, one],
        out_specs=[row, row],
        out_shape=[jax.ShapeDtypeStruct((t, LANES), F32)] * 2,
        compiler_params=_cparams(("parallel",)),
        name="rope_tables",
    )(pos_b, freq_row, sign_row)


def _rope128(v, cos, sin, half):
    lane = lax.broadcasted_iota(I32, v.shape, 1)
    first = (lane & (2 * half - 1)) < half
    fwd = pltpu.roll(v, LANES - half, 1)
    bwd = pltpu.roll(v, half, 1)
    return v * cos + jnp.where(first, fwd, bwd) * sin


def _proj_kernel(*refs, groups, has_rope):
    x_ref, w_ref = refs[0], refs[1]
    nin = 4 if has_rope else 2
    outs = refs[nin:]
    x = x_ref[...].astype(BF16)
    for (start, width, out_width, rope_half), o_ref in zip(groups, outs):
        acc = jnp.dot(x, w_ref[:, start:start + width], preferred_element_type=F32)
        if rope_half:
            cos = refs[2][...]
            sin = refs[3][...]
            for c in range(width // LANES):
                r = _rope128(acc[:, c * LANES:(c + 1) * LANES], cos, sin, rope_half)
                lo = c * LANES
                hi = min(lo + LANES, out_width)
                o_ref[:, lo:hi] = r[:, :hi - lo].astype(o_ref.dtype)
        else:
            o_ref[...] = acc[:, :out_width].astype(o_ref.dtype)


def _proj(x, w, groups, out_dtypes, rope=None, tm=512, name="proj"):
    m, k = x.shape
    n = w.shape[1]
    tm = min(tm, m)
    in_specs = [pl.BlockSpec((tm, k), lambda i: (i, 0)), pl.BlockSpec((k, n), lambda i: (0, 0))]
    args = [x, w]
    if rope is not None:
        in_specs += [pl.BlockSpec((tm, LANES), lambda i: (i, 0))] * 2
        args += list(rope)
    out_specs = [pl.BlockSpec((tm, g[2]), lambda i: (i, 0)) for g in groups]
    out_shape = [jax.ShapeDtypeStruct((m, g[2]), dt) for g, dt in zip(groups, out_dtypes)]
    return pl.pallas_call(
        functools.partial(_proj_kernel, groups=tuple(groups), has_rope=rope is not None),
        grid=(m // tm,),
        in_specs=in_specs,
        out_specs=out_specs,
        out_shape=out_shape,
        compiler_params=_cparams(("parallel",)),
        name=name,
    )(*args)


def _sb_kernel(q_ref, k_ref, v_ref, o_ref, *, tq, ck, nh, scale):
    i = pl.program_id(2)
    r = lax.broadcasted_iota(I32, (ck, ck), 0)
    c = lax.broadcasted_iota(I32, (ck, ck), 1)
    suffix = (r >= c).astype(BF16)
    row = lax.broadcasted_iota(I32, (tq, ck), 0)
    col = lax.broadcasted_iota(I32, (tq, ck), 1)

    def head_block(hh, start, carry, diag):
        tail, acc = carry
        hs = slice(hh * LANES, (hh + 1) * LANES)
        kb = k_ref[pl.ds(start, tq), hs]
        z = lax.dot_general(q_ref[:, hs], kb, _NT, preferred_element_type=F32) * scale
        for cc in reversed(range(tq // ck)):
            zc = z[:, cc * ck:(cc + 1) * ck]
            sp = jnp.maximum(zc, 0.0) + jnp.log(1.0 + jnp.exp(-jnp.abs(zc)))
            if diag:
                strict = col + cc * ck < row
                sp = jnp.where(strict, sp, 0.0)
            hi = sp.astype(BF16)
            lo = (sp - hi.astype(F32)).astype(BF16)
            cum = (jnp.dot(hi, suffix, preferred_element_type=F32)
                   + jnp.dot(lo, suffix, preferred_element_type=F32))
            a = jnp.exp(zc - cum - tail)
            if diag:
                a = jnp.where(strict, a, 0.0)
            vb = v_ref[pl.ds(pl.multiple_of(start + cc * ck, ck), ck), hs]
            acc = acc + jnp.dot(a.astype(BF16), vb, preferred_element_type=F32)
            tail = tail + jnp.sum(sp, axis=-1, keepdims=True)
        return tail, acc

    def block(j, carries, diag):
        start = pl.multiple_of(j * tq, tq)
        return tuple(head_block(hh, start, carries[hh], diag) for hh in range(nh))

    carries = tuple((jnp.zeros((tq, 1), F32), jnp.zeros((tq, LANES), F32)) for _ in range(nh))
    carries = block(i, carries, True)
    carries = lax.fori_loop(0, i, lambda jj, c: block(i - 1 - jj, c, False), carries)
    for hh in range(nh):
        o_ref[:, hh * LANES:(hh + 1) * LANES] = carries[hh][1].astype(o_ref.dtype)


def _sb_attention(h, bsz, seq, tq=512, ck=256, nh=2):
    tq = min(tq, seq)
    ck = min(ck, tq)
    nq = seq // tq
    t = bsz * seq
    hg = SB_HEADS // nh
    wide = nh * LANES
    return pl.pallas_call(
        functools.partial(_sb_kernel, tq=tq, ck=ck, nh=nh, scale=LANES ** -0.5),
        grid=(bsz, hg, nq),
        in_specs=[
            pl.BlockSpec((tq, wide), lambda b, hh, i: (b * nq + i, hh)),
            pl.BlockSpec((seq, wide), lambda b, hh, i: (b, hg + hh)),
            pl.BlockSpec((seq, wide), lambda b, hh, i: (b, 2 * hg + hh)),
        ],
        out_specs=pl.BlockSpec((tq, wide), lambda b, hh, i: (b * nq + i, hh)),
        out_shape=jax.ShapeDtypeStruct((t, SB_HEADS * LANES), BF16),
        compiler_params=_cparams(("parallel", "parallel", "arbitrary")),
        name="sb_attention",
    )(h, h, h)


def _conv_kernel(a_ref, g_ref, ap_ref, gp_ref, w_ref, cb_ref, ng_ref, nb_ref, o_ref, hext_ref, y_ref,
                 *, tm, halo, tiles_per_seq):
    i = pl.program_id(0)
    hext_ref[halo:halo + tm, :] = a_ref[...] * _sigmoid(g_ref[...])
    prev = ap_ref[...] * _sigmoid(gp_ref[...])
    first = (i % tiles_per_seq) == 0
    hext_ref[0:halo, :] = jnp.where(first, 0.0, prev)
    off = halo - (CONV_WIDTH - 1)
    for c in range(D_MODEL // LANES):
        cs = slice(c * LANES, (c + 1) * LANES)
        acc = jnp.zeros((tm, LANES), F32)
        for j in range(CONV_WIDTH):
            acc = acc + w_ref[j:j + 1, cs] * hext_ref[off + j:off + j + tm, cs]
        y_ref[:, cs] = acc
    y = y_ref[...] + cb_ref[...]
    y = _layer_norm(y, ng_ref[...], nb_ref[...])
    o_ref[...] = (y * _sigmoid(y)).astype(o_ref.dtype)


def _conformer_conv(a, g, conv_w, conv_b, norm_g, norm_b, seq, tm=256, halo=32):
    t = a.shape[0]
    tm = min(tm, seq)
    r = tm // halo
    cur = pl.BlockSpec((tm, D_MODEL), lambda i: (i, 0))
    prv = pl.BlockSpec((halo, D_MODEL), lambda i: (jnp.maximum(i * r - 1, 0), 0))
    vec = pl.BlockSpec((1, D_MODEL), lambda i: (0, 0))
    return pl.pallas_call(
        functools.partial(_conv_kernel, tm=tm, halo=halo, tiles_per_seq=seq // tm),
        grid=(t // tm,),
        in_specs=[cur, cur, prv, prv, pl.BlockSpec((CONV_WIDTH, D_MODEL), lambda i: (0, 0)), vec, vec, vec],
        out_specs=cur,
        out_shape=jax.ShapeDtypeStruct((t, D_MODEL), BF16),
        scratch_shapes=[pltpu.VMEM((tm + halo, D_MODEL), F32), pltpu.VMEM((tm, D_MODEL), F32)],
        compiler_params=_cparams(("parallel",)),
        name="conformer_conv",
    )(a, g, a, g, conv_w, conv_b.reshape(1, -1), norm_g.reshape(1, -1), norm_b.reshape(1, -1))


def _dsa_kernel(q_ref, qi_ref, wi_ref, k_ref, v_ref, ki_ref, o_ref,
                qs_ref, qis_ref, keys_ref, eq_hi_ref, mrun_ref, mb_ref, acc_ref,
                *, tq, tkb, topk, seq, idx_bits, scale):
    i = pl.program_id(1)
    nkb = (i * tq + tq + tkb - 1) // tkb
    for h in range(DSA_HEADS):
        qs_ref[h * tq:(h + 1) * tq, :] = q_ref[:, h * DSA_HEAD_DIM:(h + 1) * DSA_HEAD_DIM]
    for h in range(IDX_HEADS):
        qis_ref[h * tq:(h + 1) * tq, :] = qi_ref[:, h * IDX_DIM:(h + 1) * IDX_DIM]
    wi = wi_ref[...] * (IDX_HEADS ** -0.5 * IDX_DIM ** -0.5)
    wcols = [wi[:, h:h + 1] for h in range(IDX_HEADS)]
    q_pos = i * tq + lax.broadcasted_iota(I32, (tq, 1), 0)
    col0 = lax.broadcasted_iota(I32, (tq, tkb), 1)

    def scores(jb, _):
        start = pl.multiple_of(jb * tkb, tkb)
        kib = ki_ref[pl.ds(start, tkb), :]
        dots = lax.dot_general(qis_ref[...], kib, _NT, preferred_element_type=F32)
        d3 = dots.reshape(IDX_HEADS, tq, tkb)
        sc = wcols[0] * jnp.maximum(d3[0], 0.0)
        for h in range(1, IDX_HEADS):
            sc = sc + wcols[h] * jnp.maximum(d3[h], 0.0)
        sc = sc + 0.0
        bits = lax.bitcast_convert_type(sc, I32)
        key = bits ^ ((bits >> 31) & 0x7FFFFFFF)
        key = jnp.where(col0 + start <= q_pos, key, INT_MIN)
        keys_ref[:, pl.ds(start, tkb)] = key
        return 0

    lax.fori_loop(0, nkb, scores, 0)

    def count_if(pred):
        def body(c, acc):
            start = pl.multiple_of(c * tkb, tkb)
            ind = jnp.where(pred(keys_ref[:, pl.ds(start, tkb)], start), 1.0, 0.0)
            for s in range(tkb // LANES):
                acc = acc + ind[:, s * LANES:(s + 1) * LANES]
            return acc
        acc = lax.fori_loop(0, nkb, body, jnp.zeros((tq, LANES), F32))
        return jnp.sum(acc, axis=-1, keepdims=True)

    kf = float(topk)
    tau = jnp.where(count_if(lambda kb, st: kb >= 0) >= kf, 0, INT_MIN).astype(I32)

    def bisect(it, tau):
        cand = tau + jnp.left_shift(jnp.int32(1), 30 - it)
        return jnp.where(count_if(lambda kb, st: kb >= cand) >= kf, cand, tau)

    tau = lax.fori_loop(0, 31, bisect, tau)
    short = tau == INT_MIN
    n_gt = count_if(lambda kb, st: kb > tau)
    n_ge = count_if(lambda kb, st: kb >= tau)
    need = kf - n_gt
    eq_hi_ref[...] = jnp.where(short, -1, seq).astype(I32)
    tied = jnp.max(jnp.where((n_ge > kf) & jnp.logical_not(short), 1.0, 0.0)) > 0.0

    @pl.when(tied)
    def _():
        def step(it, p):
            cand = p + jnp.left_shift(jnp.int32(1), idx_bits - it)
            cnt = count_if(lambda kb, st: (kb == tau) & (col0 + st < cand))
            return jnp.where((cand <= seq) & (cnt < need), cand, p)

        p = lax.fori_loop(0, idx_bits + 1, step, jnp.zeros((tq, 1), I32))
        eq_hi_ref[...] = jnp.where(short, -1, p)

    eq_hi = eq_hi_ref[...]

    def selected(start):
        kb = keys_ref[:, pl.ds(start, tkb)]
        return (kb > tau) | ((kb == tau) & (col0 + start <= eq_hi))

    mrun_ref[...] = jnp.full(mrun_ref.shape, NEG_BIG, F32)

    def row_max(jb, _):
        start = pl.multiple_of(jb * tkb, tkb)
        s = lax.dot_general(qs_ref[...], k_ref[pl.ds(start, tkb), :], _NT, preferred_element_type=F32)
        s3 = jnp.where(selected(start)[None], s.reshape(DSA_HEADS, tq, tkb), NEG_BIG)
        m = mrun_ref[...]
        for c in range(tkb // LANES):
            m = jnp.maximum(m, s3[:, :, c * LANES:(c + 1) * LANES])
        mrun_ref[...] = m
        return 0

    lax.fori_loop(0, nkb, row_max, 0)
    m_row = jnp.max(mrun_ref[...], axis=-1, keepdims=True) * scale
    mb_ref[...] = jnp.broadcast_to(m_row, mb_ref.shape)

    acc_ref[...] = jnp.zeros(acc_ref.shape, F32)
    vlane = lax.broadcasted_iota(I32, (tkb, LANES), 1)

    def attend(jb, _):
        start = pl.multiple_of(jb * tkb, tkb)
        kb = k_ref[pl.ds(start, tkb), :]
        vb = jnp.where(vlane < DSA_HEAD_DIM, v_ref[pl.ds(start, tkb), :], 1.0).astype(BF16)
        s = lax.dot_general(qs_ref[...], kb, _NT, preferred_element_type=F32)
        t3 = s.reshape(DSA_HEADS, tq, tkb) * scale - mb_ref[...]
        p = jnp.where(selected(start)[None], jnp.exp(t3), 0.0)
        acc_ref[...] += jnp.dot(p.reshape(DSA_HEADS * tq, tkb).astype(BF16), vb, preferred_element_type=F32)
        return 0

    lax.fori_loop(0, nkb, attend, 0)
    acc = acc_ref[...]
    out = acc / pltpu.roll(acc, LANES - DSA_HEAD_DIM, 1)
    for h in range(DSA_HEADS):
        o_ref[:, h * DSA_HEAD_DIM:(h + 1) * DSA_HEAD_DIM] = (
            out[h * tq:(h + 1) * tq, :DSA_HEAD_DIM].astype(o_ref.dtype))


def _dsa_attention(q, qi, wi, k, v, ki, bsz, seq, tq=128, tkb=256):
    t = bsz * seq
    tq = min(tq, seq)
    tkb = min(tkb, seq)
    nq = seq // tq
    topk = min(IDX_TOPK_MAX, seq // 4)
    idx_bits = max(seq - 1, 1).bit_length()
    qrow = lambda width: pl.BlockSpec((tq, width), lambda b, i: (b * nq + i, 0))
    kv = lambda width: pl.BlockSpec((seq, width), lambda b, i: (b, 0))
    return pl.pallas_call(
        functools.partial(_dsa_kernel, tq=tq, tkb=tkb, topk=topk, seq=seq, idx_bits=idx_bits,
                          scale=DSA_HEAD_DIM ** -0.5),
        grid=(bsz, nq),
        in_specs=[qrow(DSA_HEADS * DSA_HEAD_DIM), qrow(IDX_HEADS * IDX_DIM), qrow(LANES),
                  kv(DSA_HEAD_DIM), kv(LANES), kv(IDX_DIM)],
        out_specs=qrow(DSA_HEADS * DSA_HEAD_DIM),
        out_shape=jax.ShapeDtypeStruct((t, DSA_HEADS * DSA_HEAD_DIM), BF16),
        scratch_shapes=[
            pltpu.VMEM((DSA_HEADS * tq, DSA_HEAD_DIM), BF16),
            pltpu.VMEM((IDX_HEADS * tq, IDX_DIM), BF16),
            pltpu.VMEM((tq, seq), I32),
            pltpu.VMEM((tq, 1), I32),
            pltpu.VMEM((DSA_HEADS, tq, LANES), F32),
            pltpu.VMEM((DSA_HEADS, tq, tkb), F32),
            pltpu.VMEM((DSA_HEADS * tq, LANES), F32),
        ],
        compiler_params=_cparams(("parallel", "arbitrary")),
        name="dsa_attention",
    )(q, qi, wi, k, v, ki)


def _mla_prep_kernel(cq_ref, ckv_ref, kr_ref, cos_ref, sin_ref, qg_ref, kvg_ref, wq_ref, wk_ref, wv_ref,
                     q_ref, k_ref, v_ref):
    def rms(c, g):
        return c * lax.rsqrt(jnp.mean(c * c, axis=-1, keepdims=True) + RMS_EPS) * g

    cos = cos_ref[...]
    sin = sin_ref[...]
    qn = rms(cq_ref[...], qg_ref[...]).astype(BF16)
    kvn = rms(ckv_ref[...], kvg_ref[...]).astype(BF16)
    kr = kr_ref[...]
    for h in range(MLA_HEADS):
        hs = slice(h * LANES, (h + 1) * LANES)
        qh = jnp.dot(qn, wq_ref[:, hs], preferred_element_type=F32)
        q_ref[:, hs] = _rope128(qh, cos, sin, MLA_ROPE // 2).astype(q_ref.dtype)
        kh = jnp.dot(kvn, wk_ref[:, hs], preferred_element_type=F32)
        k_ref[:, hs] = (kh + kr).astype(k_ref.dtype)
        v_ref[:, hs] = jnp.dot(kvn, wv_ref[:, hs], preferred_element_type=F32).astype(v_ref.dtype)


def _mla_prep(c_q, c_kv, kr, cos, sin, q_norm_g, kv_norm_g, wq, wk, wv, tm=512):
    t = c_q.shape[0]
    tm = min(tm, t)
    row = lambda width: pl.BlockSpec((tm, width), lambda i: (i, 0))
    full = lambda a: pl.BlockSpec(a.shape, lambda i: (0, 0))
    qg = q_norm_g.reshape(1, -1)
    kvg = kv_norm_g.reshape(1, -1)
    width = MLA_HEADS * LANES
    return pl.pallas_call(
        _mla_prep_kernel,
        grid=(t // tm,),
        in_specs=[row(MLA_Q_RANK), row(MLA_KV_RANK), row(LANES), row(LANES), row(LANES),
                  full(qg), full(kvg), full(wq), full(wk), full(wv)],
        out_specs=[row(width)] * 3,
        out_shape=[jax.ShapeDtypeStruct((t, width), BF16)] * 3,
        compiler_params=_cparams(("parallel",)),
        name="mla_prep",
    )(c_q, c_kv, kr, cos, sin, qg, kvg, wq, wk, wv)


def _flash_kernel(q_ref, k_ref, v_ref, o_ref, *, tq, nh, scale):
    i = pl.program_id(2)
    row = lax.broadcasted_iota(I32, (tq, tq), 0)
    col = lax.broadcasted_iota(I32, (tq, tq), 1)
    causal = col <= row
    ones = jnp.ones((tq, LANES), BF16)

    def head_block(hh, start, carry, diag):
        m, acc = carry
        hs = slice(hh * LANES, (hh + 1) * LANES)
        kb = k_ref[pl.ds(start, tq), hs]
        vb = jnp.concatenate([v_ref[pl.ds(start, tq), hs], ones], axis=1)
        s = lax.dot_general(q_ref[:, hs], kb, _NT, preferred_element_type=F32) * scale
        if diag:
            s = jnp.where(causal, s, NEG_BIG)
        m_new = jnp.maximum(m, jnp.max(s, axis=-1, keepdims=True))
        alpha = jnp.exp(m - m_new)
        p = jnp.exp(s - m_new)
        acc = alpha * acc + jnp.dot(p.astype(BF16), vb, preferred_element_type=F32)
        return m_new, acc

    def block(j, carries, diag):
        start = pl.multiple_of(j * tq, tq)
        return tuple(head_block(hh, start, carries[hh], diag) for hh in range(nh))

    carries = tuple((jnp.full((tq, 1), NEG_BIG, F32), jnp.zeros((tq, 2 * LANES), F32)) for _ in range(nh))
    carries = block(i, carries, True)
    carries = lax.fori_loop(0, i, lambda jj, c: block(i - 1 - jj, c, False), carries)
    for hh in range(nh):
        acc = carries[hh][1]
        o_ref[:, hh * LANES:(hh + 1) * LANES] = (acc[:, :LANES] / acc[:, LANES:]).astype(o_ref.dtype)


def _causal_attention(q, k, v, bsz, seq, heads, scale, tq=512, nh=2):
    tq = min(tq, seq)
    nq = seq // tq
    t = bsz * seq
    wide = nh * LANES
    kv = pl.BlockSpec((seq, wide), lambda b, hh, i: (b, hh))
    qo = pl.BlockSpec((tq, wide), lambda b, hh, i: (b * nq + i, hh))
    return pl.pallas_call(
        functools.partial(_flash_kernel, tq=tq, nh=nh, scale=scale),
        grid=(bsz, heads // nh, nq),
        in_specs=[qo, kv, kv],
        out_specs=qo,
        out_shape=jax.ShapeDtypeStruct((t, heads * LANES), BF16),
        compiler_params=_cparams(("parallel", "parallel", "arbitrary")),
        name="mla_attention",
    )(q, k, v)


def _route(sb, s):
    per = N_EXPERTS // N_GROUPS
    gscore = []
    for g in range(N_GROUPS):
        a, b, c, d = sb[per * g:per * g + per]
        hi1, lo1 = jnp.maximum(a, b), jnp.minimum(a, b)
        hi2, lo2 = jnp.maximum(c, d), jnp.minimum(c, d)
        top1 = jnp.maximum(hi1, hi2)
        top2 = jnp.maximum(jnp.minimum(hi1, hi2), jnp.maximum(lo1, lo2))
        gscore.append(top1 + top2)
    best, gidx = gscore[0], jnp.zeros(gscore[0].shape, I32)
    for g in range(1, N_GROUPS):
        upd = gscore[g] > best
        best = jnp.where(upd, gscore[g], best)
        gidx = jnp.where(upd, g, gidx)

    def pick(rows, j):
        out = rows[j]
        for g in range(1, N_GROUPS):
            out = jnp.where(gidx == g, rows[per * g + j], out)
        return out

    vals = [pick(sb, j) for j in range(per)]
    raws = [pick(s, j) for j in range(per)]
    b1, i1, s1 = vals[0], jnp.zeros(gidx.shape, I32), raws[0]
    for j in range(1, per):
        upd = vals[j] > b1
        b1 = jnp.where(upd, vals[j], b1)
        i1 = jnp.where(upd, j, i1)
        s1 = jnp.where(upd, raws[j], s1)
    b2, i2, s2 = jnp.zeros_like(b1), jnp.full(gidx.shape, -1, I32), jnp.zeros_like(b1)
    for j in range(per):
        upd = (i1 != j) & ((i2 < 0) | (vals[j] > b2))
        b2 = jnp.where(upd, vals[j], b2)
        i2 = jnp.where(upd, j, i2)
        s2 = jnp.where(upd, raws[j], s2)
    denom = s1 + s2
    w1 = s1 / denom
    w2 = s2 / denom
    gates = []
    for e in range(N_EXPERTS):
        in_group = gidx == (e // per)
        gates.append(jnp.where(in_group & (i1 == e % per), w1,
                               jnp.where(in_group & (i2 == e % per), w2, 0.0)))
    return gates


def _post_kernel(x_ref, mix_ref, mq_ref, kvm_ref, wo_ref, g_ref, b_ref, rw_ref, rb_ref, x1_ref, gate_ref):
    mq = mq_ref[...]
    kvm = kvm_ref[0]
    mem = []
    for h in range(MEM_HEADS):
        hs = slice(h * MEM_HEAD_DIM, (h + 1) * MEM_HEAD_DIM)
        vs = slice(MEM_WIDTH + h * MEM_HEAD_DIM, MEM_WIDTH + (h + 1) * MEM_HEAD_DIM)
        s = lax.dot_general(mq[:, hs], kvm[:, hs], _NT, preferred_element_type=F32) * (MEM_HEAD_DIM ** -0.5)
        p = jnp.exp(s - jnp.max(s, axis=-1, keepdims=True))
        p = p / jnp.sum(p, axis=-1, keepdims=True)
        mem.append(jnp.dot(p.astype(BF16), kvm[:, vs], preferred_element_type=F32))
    y = jnp.dot(mix_ref[...], wo_ref[0:D_MODEL, :], preferred_element_type=F32)
    for h in range(MEM_HEADS):
        r0 = D_MODEL + h * MEM_HEAD_DIM
        y = y + jnp.dot(mem[h].astype(BF16), wo_ref[r0:r0 + MEM_HEAD_DIM, :], preferred_element_type=F32)
    x1 = _layer_norm(DEEPNORM_ALPHA * x_ref[...] + y, g_ref[...], b_ref[...])
    x1_ref[...] = x1
    logits = lax.dot_general(rw_ref[...], x1.astype(BF16), _NT, preferred_element_type=F32)
    s_all = _sigmoid(logits)
    sb_all = s_all + rb_ref[...]
    gates = _route([sb_all[e:e + 1, :] for e in range(N_EXPERTS)], [s_all[e:e + 1, :] for e in range(N_EXPERTS)])
    for e in range(N_EXPERTS):
        gate_ref[e:e + 1, :] = gates[e]


def _post_mixer(x, mix, mq, kvm, w_o, ln_g, ln_b, rw_t, rbias, seq, tm=256):
    t = x.shape[0]
    tm = min(tm, seq)
    per_seq = seq // tm
    row = lambda width: pl.BlockSpec((tm, width), lambda i: (i, 0))
    full = lambda a: pl.BlockSpec(a.shape, lambda i: (0,) * a.ndim)
    g = ln_g.reshape(1, -1)
    b = ln_b.reshape(1, -1)
    return pl.pallas_call(
        _post_kernel,
        grid=(t // tm,),
        in_specs=[row(D_MODEL), row(D_MODEL), row(MEM_WIDTH),
                  pl.BlockSpec((1,) + kvm.shape[1:], lambda i: (i // per_seq, 0, 0)),
                  full(w_o), full(g), full(b), full(rw_t), full(rbias)],
        out_specs=[row(D_MODEL), pl.BlockSpec((N_EXPERTS, tm), lambda i: (0, i))],
        out_shape=[jax.ShapeDtypeStruct((t, D_MODEL), F32), jax.ShapeDtypeStruct((N_EXPERTS, t), F32)],
        compiler_params=_cparams(("parallel",)),
        name="post_mixer",
    )(x, mix, mq, kvm, w_o, g, b, rw_t, rbias)


def _moe_kernel(x_ref, gate_ref, win_ref, wout_ref, g_ref, b_ref, o_ref, xb_ref, y_ref):
    e = pl.program_id(1)

    @pl.when(e == 0)
    def _():
        xb_ref[...] = x_ref[...].astype(BF16)
        y_ref[...] = jnp.zeros(y_ref.shape, F32)

    h = jnp.dot(xb_ref[...], win_ref[0], preferred_element_type=F32)
    gate_in = h[:, :EXPERT_FF]
    act = gate_in * _sigmoid(gate_in) * h[:, EXPERT_FF:]
    yb = jnp.dot(act.astype(BF16), wout_ref[0], preferred_element_type=F32)
    gates = gate_ref[...]
    lane = lax.broadcasted_iota(I32, gates.shape, 1)
    gcol = jnp.sum(jnp.where(lane == e, gates, 0.0), axis=-1, keepdims=True)
    y_ref[...] += yb * gcol

    @pl.when(e == N_EXPERTS - 1)
    def _():
        o_ref[...] = _layer_norm(DEEPNORM_ALPHA * x_ref[...] + y_ref[...], g_ref[...], b_ref[...])


def _moe(x1, gates, w_in, w_out, ln_g, ln_b, tm=512):
    t = x1.shape[0]
    tm = min(tm, t)
    g = ln_g.reshape(1, -1)
    b = ln_b.reshape(1, -1)
    vec = pl.BlockSpec((1, D_MODEL), lambda i, e: (0, 0))
    return pl.pallas_call(
        _moe_kernel,
        grid=(t // tm, N_EXPERTS),
        in_specs=[pl.BlockSpec((tm, D_MODEL), lambda i, e: (i, 0)),
                  pl.BlockSpec((tm, LANES), lambda i, e: (i, 0)),
                  pl.BlockSpec((1, D_MODEL, 2 * EXPERT_FF), lambda i, e: (e, 0, 0)),
                  pl.BlockSpec((1, EXPERT_FF, D_MODEL), lambda i, e: (e, 0, 0)),
                  vec, vec],
        out_specs=pl.BlockSpec((tm, D_MODEL), lambda i, e: (i, 0)),
        out_shape=jax.ShapeDtypeStruct((t, D_MODEL), F32),
        scratch_shapes=[pltpu.VMEM((tm, D_MODEL), BF16), pltpu.VMEM((tm, D_MODEL), F32)],
        compiler_params=_cparams(("parallel", "arbitrary")),
        name="moe_ffn",
    )(x1, gates, w_in, w_out, g, b)


def _pad_cols(w, width, offset=0):
    out = jnp.zeros((w.shape[0], width), w.dtype)
    return out.at[:, offset:offset + w.shape[1]].set(w)


def _rope_rows(half, lane_lo, lane_hi):
    inv_freq = ROPE_THETA ** (-jnp.arange(half, dtype=F32) / half)
    lane = jnp.arange(LANES)
    active = (lane >= lane_lo) & (lane < lane_hi)
    pos_in = (lane - lane_lo) % (2 * half)
    freq = jnp.where(active, inv_freq[pos_in % half], 0.0).astype(F32)
    sign = jnp.where(active, jnp.where(pos_in < half, -1.0, 1.0), 0.0).astype(F32)
    return freq.reshape(1, LANES), sign.reshape(1, LANES)


def kernel(x, mem, positions, router_w, router_bias, l0_w_in, l0_mem_wkv, l0_w_o, l0_ln_g, l0_ln_b, l0_moe_w_in, l0_moe_w_out, l1_w_in, l1_conv_w, l1_conv_b, l1_conv_norm_g, l1_conv_norm_b, l1_mem_wkv, l1_w_o, l1_ln_g, l1_ln_b, l1_moe_w_in, l1_moe_w_out, l2_w_in, l2_mem_wkv, l2_w_o, l2_ln_g, l2_ln_b, l2_moe_w_in, l2_moe_w_out, l3_w_in, l3_q_norm_g, l3_kv_norm_g, l3_w_q_up, l3_w_kv_up, l3_mem_wkv, l3_w_o, l3_ln_g, l3_ln_b, l3_moe_w_in, l3_moe_w_out):
    bsz, seq, d = x.shape
    t = bsz * seq
    n_mem = mem.shape[1]
    xt = x.reshape(t, d)
    mem2 = mem.reshape(bsz * n_mem, d)
    pos_b = jnp.broadcast_to(positions.astype(F32).reshape(t, 1), (t, LANES))
    rw_t = router_w.T.astype(BF16)
    rbias = router_bias.astype(F32).reshape(N_EXPERTS, 1)

    def finish_layer(xt, mix, mq, mem_wkv, w_o, ln_g, ln_b, moe_w_in, moe_w_out):
        (kvm,) = _proj(mem2, mem_wkv.astype(BF16), [(0, 2 * MEM_WIDTH, 2 * MEM_WIDTH, 0)], [BF16],
                       tm=n_mem, name="mem_kv")
        kvm = kvm.reshape(bsz, n_mem, 2 * MEM_WIDTH)
        x1, gates_t = _post_mixer(xt, mix, mq, kvm, w_o.astype(BF16), ln_g[0], ln_b[0], rw_t, rbias, seq)
        gates = _pad_cols(gates_t.T, LANES)
        return _moe(x1, gates, moe_w_in.astype(BF16), moe_w_out.astype(BF16), ln_g[1], ln_b[1])

    w0 = l0_w_in.astype(BF16)
    n0 = w0.shape[1]
    (h0,) = _proj(xt, w0, [(0, n0, n0, 0)], [BF16], name="proj0")
    mix = _sb_attention(h0, bsz, seq)
    mq = h0[:, 3 * D_MODEL:]
    xt = finish_layer(xt, mix, mq, l0_mem_wkv, l0_w_o, l0_ln_g, l0_ln_b, l0_moe_w_in, l0_moe_w_out)

    w1 = l1_w_in.astype(BF16)
    a, g, mq = _proj(xt, w1, [(0, D_MODEL, D_MODEL, 0), (D_MODEL, D_MODEL, D_MODEL, 0),
                              (2 * D_MODEL, MEM_WIDTH, MEM_WIDTH, 0)], [F32, F32, BF16], name="proj1")
    mix = _conformer_conv(a, g, l1_conv_w, l1_conv_b, l1_conv_norm_g, l1_conv_norm_b, seq)
    xt = finish_layer(xt, mix, mq, l1_mem_wkv, l1_w_o, l1_ln_g, l1_ln_b, l1_moe_w_in, l1_moe_w_out)

    w2 = l2_w_in
    c = 0
    parts = []
    for width in (DSA_HEADS * DSA_HEAD_DIM, DSA_HEAD_DIM, DSA_HEAD_DIM, IDX_HEADS * IDX_DIM, IDX_DIM, IDX_HEADS,
                  MEM_WIDTH):
        parts.append(w2[:, c:c + width])
        c += width
    wq, wk, wv, wqi, wki, wwi, wmq = parts
    w2p = jnp.concatenate([wq, _pad_cols(wk, LANES), _pad_cols(wv, LANES), wqi, _pad_cols(wki, LANES),
                           _pad_cols(wwi, LANES), wmq], axis=1).astype(BF16)
    freq, sign = _rope_rows(DSA_HEAD_DIM // 2, 0, LANES)
    cos_a, sin_a = _rope_tables(pos_b, freq, sign)
    rh = DSA_HEAD_DIM // 2
    groups = [(0, 1024, 1024, rh), (1024, LANES, DSA_HEAD_DIM, rh), (1152, LANES, LANES, 0),
              (1280, 512, 512, rh), (1792, LANES, IDX_DIM, rh), (1920, LANES, LANES, 0),
              (2048, MEM_WIDTH, MEM_WIDTH, 0)]
    q, k, v, qi, ki, wi, mq = _proj(xt, w2p, groups, [BF16, BF16, BF16, BF16, BF16, F32, BF16],
                                    rope=(cos_a, sin_a), name="proj2")
    mix = _dsa_attention(q, qi, wi, k, v, ki, bsz, seq)
    xt = finish_layer(xt, mix, mq, l2_mem_wkv, l2_w_o, l2_ln_g, l2_ln_b, l2_moe_w_in, l2_moe_w_out)

    w3 = l3_w_in
    rope_lo = MLA_NOPE
    w_cq = w3[:, :MLA_Q_RANK]
    w_ckv = w3[:, MLA_Q_RANK:MLA_Q_RANK + MLA_KV_RANK]
    w_kr = w3[:, MLA_Q_RANK + MLA_KV_RANK:MLA_Q_RANK + MLA_KV_RANK + MLA_ROPE]
    w_mq = w3[:, MLA_Q_RANK + MLA_KV_RANK + MLA_ROPE:]
    w3p = jnp.concatenate([w_cq, w_ckv, _pad_cols(w_kr, LANES, rope_lo), w_mq], axis=1).astype(BF16)
    freq, sign = _rope_rows(MLA_ROPE // 2, rope_lo, rope_lo + MLA_ROPE)
    cos_b, sin_b = _rope_tables(pos_b, freq, sign)
    groups = [(0, MLA_Q_RANK, MLA_Q_RANK, 0), (MLA_Q_RANK, MLA_KV_RANK, MLA_KV_RANK, 0),
              (MLA_Q_RANK + MLA_KV_RANK, LANES, LANES, MLA_ROPE // 2),
              (MLA_Q_RANK + MLA_KV_RANK + LANES, MEM_WIDTH, MEM_WIDTH, 0)]
    c_q, c_kv, kr, mq = _proj(xt, w3p, groups, [F32, F32, F32, BF16], rope=(cos_b, sin_b), name="proj3")
    qk_dim = MLA_NOPE + MLA_ROPE
    wq_heads = l3_w_q_up.reshape(MLA_Q_RANK, MLA_HEADS, qk_dim)
    wq_p = jnp.zeros((MLA_Q_RANK, MLA_HEADS, LANES), F32).at[:, :, :qk_dim].set(wq_heads)
    wkv_heads = l3_w_kv_up.reshape(MLA_KV_RANK, MLA_HEADS, MLA_NOPE + MLA_V)
    wk_p = jnp.zeros((MLA_KV_RANK, MLA_HEADS, LANES), F32).at[:, :, :MLA_NOPE].set(wkv_heads[:, :, :MLA_NOPE])
    wv_p = wkv_heads[:, :, MLA_NOPE:]
    flat = lambda w: w.reshape(w.shape[0], MLA_HEADS * LANES).astype(BF16)
    qh, kh, vh = _mla_prep(c_q, c_kv, kr, cos_b, sin_b, l3_q_norm_g, l3_kv_norm_g, flat(wq_p), flat(wk_p),
                           flat(wv_p))
    mix = _causal_attention(qh, kh, vh, bsz, seq, MLA_HEADS, qk_dim ** -0.5)
    xt = finish_layer(xt, mix, mq, l3_mem_wkv, l3_w_o, l3_ln_g, l3_ln_b, l3_moe_w_in, l3_moe_w_out)
    return xt.reshape(bsz, seq, d)
```

```python
import functools

import jax
import jax.numpy as jnp
from jax import lax
from jax.experimental import pallas as pl
from jax.experimental.pallas import tpu as pltpu

F32 = jnp.float32
BF16 = jnp.bfloat16
I32 = jnp.int32

D_MODEL = 1024
DEPTH = 4
ROPE_THETA = 10000.0
LN_EPS = 1e-5
RMS_EPS = 1e-6
DEEPNORM_ALPHA = (2.0 * DEPTH) ** 0.25
SB_HEADS = 8
CONV_WIDTH = 31
DSA_HEADS = 16
DSA_HEAD_DIM = 64
IDX_HEADS = 8
IDX_DIM = 64
IDX_TOPK_MAX = 256
MLA_HEADS = 8
MLA_Q_RANK = 384
MLA_KV_RANK = 256
MLA_NOPE = 64
MLA_ROPE = 32
MLA_V = 128
MEM_HEADS = 4
MEM_HEAD_DIM = 64
MEM_WIDTH = 256
N_EXPERTS = 16
N_GROUPS = 4
EXPERT_FF = 512

LANES = 128
VMEM_LIMIT_BYTES = 56 * 1024 * 1024
NEG_BIG = -1e30
MAX_SAFE_SHIFT = 40.0
INT_MIN = -2 ** 31

_NT = (((1,), (1,)), ((), ()))


def _cparams(sem):
    return pltpu.CompilerParams(dimension_semantics=sem, vmem_limit_bytes=VMEM_LIMIT_BYTES)


def _layer_norm(v, g, b):
    mu = jnp.mean(v, axis=-1, keepdims=True)
    d = v - mu
    var = jnp.mean(d * d, axis=-1, keepdims=True)
    return d * lax.rsqrt(var + LN_EPS) * g + b


def _sigmoid(v):
    return 1.0 / (1.0 + jnp.exp(-v))


def _rope_table_kernel(pos_ref, freq_ref, sign_ref, cos_ref, sin_ref):
    ang = pos_ref[...] * freq_ref[...]
    cos_ref[...] = jnp.cos(ang)
    sin_ref[...] = jnp.sin(ang) * sign_ref[...]


def _rope_tables(pos_b, freq_row, sign_row, tm=512):
    t = pos_b.shape[0]
    row = pl.BlockSpec((tm, LANES), lambda i: (i, 0))
    one = pl.BlockSpec((1, LANES), lambda i: (0, 0))
    return pl.pallas_call(
        _rope_table_kernel,
        grid=(t // tm,),
        in_specs=[row, one, one],
        out_specs=[row, row],
        out_shape=[jax.ShapeDtypeStruct((t, LANES), F32)] * 2,
        compiler_params=_cparams(("parallel",)),
        name="rope_tables",
    )(pos_b, freq_row, sign_row)


def _rope128(v, cos, sin, half):
    lane = lax.broadcasted_iota(I32, v.shape, 1)
    first = (lane & (2 * half - 1)) < half
    fwd = pltpu.roll(v, LANES - half, 1)
    bwd = pltpu.roll(v, half, 1)
    return v * cos + jnp.where(first, fwd, bwd) * sin


def _proj_kernel(*refs, groups, has_rope):
    x_ref, w_ref = refs[0], refs[1]
    nin = 4 if has_rope else 2
    outs = refs[nin:]
    x = x_ref[...].astype(BF16)
    for (start, width, out_width, rope_half), o_ref in zip(groups, outs):
        acc = jnp.dot(x, w_ref[:, start:start + width], preferred_element_type=F32)
        if rope_half:
            cos = refs[2][...]
            sin = refs[3][...]
            for c in range(width // LANES):
                r = _rope128(acc[:, c * LANES:(c + 1) * LANES], cos, sin, rope_half)
                lo = c * LANES
                hi = min(lo + LANES, out_width)
                o_ref[:, lo:hi] = r[:, :hi - lo].astype(o_ref.dtype)
        else:
            o_ref[...] = acc[:, :out_width].astype(o_ref.dtype)


def _proj(x, w, groups, out_dtypes, rope=None, tm=512, name="proj"):
    m, k = x.shape
    n = w.shape[1]
    tm = min(tm, m)
    in_specs = [pl.BlockSpec((tm, k), lambda i: (i, 0)), pl.BlockSpec((k, n), lambda i: (0, 0))]
    args = [x, w]
    if rope is not None:
        in_specs += [pl.BlockSpec((tm, LANES), lambda i: (i, 0))] * 2
        args += list(rope)
    out_specs = [pl.BlockSpec((tm, g[2]), lambda i: (i, 0)) for g in groups]
    out_shape = [jax.ShapeDtypeStruct((m, g[2]), dt) for g, dt in zip(groups, out_dtypes)]
    return pl.pallas_call(
        functools.partial(_proj_kernel, groups=tuple(groups), has_rope=rope is not None),
        grid=(m // tm,),
        in_specs=in_specs,
        out_specs=out_specs,
        out_shape=out_shape,
        compiler_params=_cparams(("parallel",)),
        name=name,
    )(*args)


def _sb_kernel(q_ref, k_ref, v_ref, o_ref, *, tq, ck, nh, scale):
    i = pl.program_id(2)
    r = lax.broadcasted_iota(I32, (ck, ck), 0)
    c = lax.broadcasted_iota(I32, (ck, ck), 1)
    suffix = (r >= c).astype(BF16)
    row = lax.broadcasted_iota(I32, (tq, ck), 0)
    col = lax.broadcasted_iota(I32, (tq, ck), 1)

    def head_block(hh, start, carry, diag):
        tail, acc = carry
        hs = slice(hh * LANES, (hh + 1) * LANES)
        kb = k_ref[pl.ds(start, tq), hs]
        z = lax.dot_general(q_ref[:, hs], kb, _NT, preferred_element_type=F32) * scale
        for cc in reversed(range(tq // ck)):
            zc = z[:, cc * ck:(cc + 1) * ck]
            sp = jnp.maximum(zc, 0.0) + jnp.log(1.0 + jnp.exp(-jnp.abs(zc)))
            if diag:
                strict = col + cc * ck < row
                sp = jnp.where(strict, sp, 0.0)
            cum = jnp.dot(sp.astype(BF16), suffix, preferred_element_type=F32)
            a = jnp.exp(zc - cum - tail)
            if diag:
                a = jnp.where(strict, a, 0.0)
            vb = v_ref[pl.ds(pl.multiple_of(start + cc * ck, ck), ck), hs]
            acc = acc + jnp.dot(a.astype(BF16), vb, preferred_element_type=F32)
            tail = tail + jnp.sum(sp, axis=-1, keepdims=True)
        return tail, acc

    def block(j, carries, diag):
        start = pl.multiple_of(j * tq, tq)
        return tuple(head_block(hh, start, carries[hh], diag) for hh in range(nh))

    carries = tuple((jnp.zeros((tq, 1), F32), jnp.zeros((tq, LANES), F32)) for _ in range(nh))
    carries = block(i, carries, True)
    carries = lax.fori_loop(0, i, lambda jj, c: block(i - 1 - jj, c, False), carries)
    for hh in range(nh):
        o_ref[:, hh * LANES:(hh + 1) * LANES] = carries[hh][1].astype(o_ref.dtype)


def _sb_attention(h, bsz, seq, tq=512, ck=256, nh=2):
    tq = min(tq, seq)
    ck = min(ck, tq)
    nq = seq // tq
    t = bsz * seq
    hg = SB_HEADS // nh
    wide = nh * LANES
    return pl.pallas_call(
        functools.partial(_sb_kernel, tq=tq, ck=ck, nh=nh, scale=LANES ** -0.5),
        grid=(bsz, hg, nq),
        in_specs=[
            pl.BlockSpec((tq, wide), lambda b, hh, i: (b * nq + i, hh)),
            pl.BlockSpec((seq, wide), lambda b, hh, i: (b, hg + hh)),
            pl.BlockSpec((seq, wide), lambda b, hh, i: (b, 2 * hg + hh)),
        ],
        out_specs=pl.BlockSpec((tq, wide), lambda b, hh, i: (b * nq + i, hh)),
        out_shape=jax.ShapeDtypeStruct((t, SB_HEADS * LANES), BF16),
        compiler_params=_cparams(("parallel", "parallel", "arbitrary")),
        name="sb_attention",
    )(h, h, h)


def _conv_kernel(a_ref, g_ref, ap_ref, gp_ref, w_ref, cb_ref, ng_ref, nb_ref, o_ref, hext_ref, y_ref,
                 *, tm, halo, tiles_per_seq):
    i = pl.program_id(0)
    hext_ref[halo:halo + tm, :] = a_ref[...] * _sigmoid(g_ref[...])
    prev = ap_ref[...] * _sigmoid(gp_ref[...])
    first = (i % tiles_per_seq) == 0
    hext_ref[0:halo, :] = jnp.where(first, 0.0, prev)
    off = halo - (CONV_WIDTH - 1)
    for c in range(D_MODEL // LANES):
        cs = slice(c * LANES, (c + 1) * LANES)
        acc = jnp.zeros((tm, LANES), F32)
        for j in range(CONV_WIDTH):
            acc = acc + w_ref[j:j + 1, cs] * hext_ref[off + j:off + j + tm, cs]
        y_ref[:, cs] = acc
    y = y_ref[...] + cb_ref[...]
    y = _layer_norm(y, ng_ref[...], nb_ref[...])
    o_ref[...] = (y * _sigmoid(y)).astype(o_ref.dtype)


def _conformer_conv(a, g, conv_w, conv_b, norm_g, norm_b, seq, tm=256, halo=32):
    t = a.shape[0]
    tm = min(tm, seq)
    r = tm // halo
    cur = pl.BlockSpec((tm, D_MODEL), lambda i: (i, 0))
    prv = pl.BlockSpec((halo, D_MODEL), lambda i: (jnp.maximum(i * r - 1, 0), 0))
    vec = pl.BlockSpec((1, D_MODEL), lambda i: (0, 0))
    return pl.pallas_call(
        functools.partial(_conv_kernel, tm=tm, halo=halo, tiles_per_seq=seq // tm),
        grid=(t // tm,),
        in_specs=[cur, cur, prv, prv, pl.BlockSpec((CONV_WIDTH, D_MODEL), lambda i: (0, 0)), vec, vec, vec],
        out_specs=cur,
        out_shape=jax.ShapeDtypeStruct((t, D_MODEL), BF16),
        scratch_shapes=[pltpu.VMEM((tm + halo, D_MODEL), F32), pltpu.VMEM((tm, D_MODEL), F32)],
        compiler_params=_cparams(("parallel",)),
        name="conformer_conv",
    )(a, g, a, g, conv_w, conv_b.reshape(1, -1), norm_g.reshape(1, -1), norm_b.reshape(1, -1))


def _dsa_kernel(q_ref, qi_ref, wi_ref, k_ref, v_ref, ki_ref, o_ref,
                qs_ref, qis_ref, keys_ref, eq_hi_ref, mrun_ref, mb_ref, acc_ref, kmax2_ref,
                *, tq, tkb, topk, seq, idx_bits, scale):
    i = pl.program_id(1)
    nkb = (i * tq + tq + tkb - 1) // tkb

    @pl.when(i == 0)
    def _():
        rows = min(tkb, seq)

        def body(c, m):
            kk = k_ref[pl.ds(pl.multiple_of(c * rows, rows), rows), :].astype(F32)
            return jnp.maximum(m, jnp.sum(kk * kk, axis=-1, keepdims=True))

        kmax2_ref[0] = jnp.max(lax.fori_loop(0, seq // rows, body, jnp.zeros((rows, 1), F32)))

    for h in range(DSA_HEADS):
        qs_ref[h * tq:(h + 1) * tq, :] = q_ref[:, h * DSA_HEAD_DIM:(h + 1) * DSA_HEAD_DIM]
    for h in range(IDX_HEADS):
        qis_ref[h * tq:(h + 1) * tq, :] = qi_ref[:, h * IDX_DIM:(h + 1) * IDX_DIM]
    wi = wi_ref[...] * (IDX_HEADS ** -0.5 * IDX_DIM ** -0.5)
    wcols = [wi[:, h:h + 1] for h in range(IDX_HEADS)]
    q_pos = i * tq + lax.broadcasted_iota(I32, (tq, 1), 0)
    col0 = lax.broadcasted_iota(I32, (tq, tkb), 1)

    def scores(jb, _):
        start = pl.multiple_of(jb * tkb, tkb)
        kib = ki_ref[pl.ds(start, tkb), :]
        dots = lax.dot_general(qis_ref[...], kib, _NT, preferred_element_type=F32)
        d3 = dots.reshape(IDX_HEADS, tq, tkb)
        sc = wcols[0] * jnp.maximum(d3[0], 0.0)
        for h in range(1, IDX_HEADS):
            sc = sc + wcols[h] * jnp.maximum(d3[h], 0.0)
        sc = sc + 0.0
        bits = lax.bitcast_convert_type(sc, I32)
        key = bits ^ ((bits >> 31) & 0x7FFFFFFF)
        key = jnp.where(col0 + start <= q_pos, key, INT_MIN)
        keys_ref[:, pl.ds(start, tkb)] = key
        return 0

    lax.fori_loop(0, nkb, scores, 0)

    def count_if(pred):
        def body(c, acc):
            start = pl.multiple_of(c * tkb, tkb)
            ind = jnp.where(pred(keys_ref[:, pl.ds(start, tkb)], start), 1.0, 0.0)
            for s in range(tkb // LANES):
                acc = acc + ind[:, s * LANES:(s + 1) * LANES]
            return acc
        acc = lax.fori_loop(0, nkb, body, jnp.zeros((tq, LANES), F32))
        return jnp.sum(acc, axis=-1, keepdims=True)

    kf = float(topk)
    tau = jnp.where(count_if(lambda kb, st: kb >= 0) >= kf, 0, INT_MIN).astype(I32)

    def bisect(it, tau):
        cand = tau + jnp.left_shift(jnp.int32(1), 30 - it)
        return jnp.where(count_if(lambda kb, st: kb >= cand) >= kf, cand, tau)

    tau = lax.fori_loop(0, 31, bisect, tau)
    short = tau == INT_MIN
    n_gt = count_if(lambda kb, st: kb > tau)
    n_ge = count_if(lambda kb, st: kb >= tau)
    need = kf - n_gt
    eq_hi_ref[...] = jnp.where(short, -1, seq).astype(I32)
    tied = jnp.max(jnp.where((n_ge > kf) & jnp.logical_not(short), 1.0, 0.0)) > 0.0

    @pl.when(tied)
    def _():
        def step(it, p):
            cand = p + jnp.left_shift(jnp.int32(1), idx_bits - it)
            cnt = count_if(lambda kb, st: (kb == tau) & (col0 + st < cand))
            return jnp.where((cand <= seq) & (cnt < need), cand, p)

        p = lax.fori_loop(0, idx_bits + 1, step, jnp.zeros((tq, 1), I32))
        eq_hi_ref[...] = jnp.where(short, -1, p)

    eq_hi = eq_hi_ref[...]

    def selected(start):
        kb = keys_ref[:, pl.ds(start, tkb)]
        return (kb > tau) | ((kb == tau) & (col0 + start <= eq_hi))

    qf = qs_ref[...].astype(F32)
    q_norm2 = jnp.sum(qf * qf, axis=-1, keepdims=True).reshape(DSA_HEADS, tq, 1)
    bound = jnp.sqrt(q_norm2 * kmax2_ref[0]) * scale
    bound_ok = jnp.max(bound) < MAX_SAFE_SHIFT

    @pl.when(bound_ok)
    def _():
        mb_ref[...] = jnp.broadcast_to(bound, mb_ref.shape)

    @pl.when(jnp.logical_not(bound_ok))
    def _():
        mrun_ref[...] = jnp.full(mrun_ref.shape, NEG_BIG, F32)

        def row_max(jb, _):
            start = pl.multiple_of(jb * tkb, tkb)
            s = lax.dot_general(qs_ref[...], k_ref[pl.ds(start, tkb), :], _NT, preferred_element_type=F32)
            s3 = jnp.where(selected(start)[None], s.reshape(DSA_HEADS, tq, tkb), NEG_BIG)
            m = mrun_ref[...]
            for c in range(tkb // LANES):
                m = jnp.maximum(m, s3[:, :, c * LANES:(c + 1) * LANES])
            mrun_ref[...] = m
            return 0

        lax.fori_loop(0, nkb, row_max, 0)
        m_row = jnp.max(mrun_ref[...], axis=-1, keepdims=True) * scale
        mb_ref[...] = jnp.broadcast_to(m_row, mb_ref.shape)

    acc_ref[...] = jnp.zeros(acc_ref.shape, F32)
    vlane = lax.broadcasted_iota(I32, (tkb, LANES), 1)

    def attend(jb, _):
        start = pl.multiple_of(jb * tkb, tkb)
        kb = k_ref[pl.ds(start, tkb), :]
        vb = jnp.where(vlane < DSA_HEAD_DIM, v_ref[pl.ds(start, tkb), :], 1.0).astype(BF16)
        s = lax.dot_general(qs_ref[...], kb, _NT, preferred_element_type=F32)
        t3 = s.reshape(DSA_HEADS, tq, tkb) * scale - mb_ref[...]
        p = jnp.where(selected(start)[None], jnp.exp(t3), 0.0)
        acc_ref[...] += jnp.dot(p.reshape(DSA_HEADS * tq, tkb).astype(BF16), vb, preferred_element_type=F32)
        return 0

    lax.fori_loop(0, nkb, attend, 0)
    acc = acc_ref[...]
    out = acc / pltpu.roll(acc, LANES - DSA_HEAD_DIM, 1)
    for h in range(DSA_HEADS):
        o_ref[:, h * DSA_HEAD_DIM:(h + 1) * DSA_HEAD_DIM] = (
            out[h * tq:(h + 1) * tq, :DSA_HEAD_DIM].astype(o_ref.dtype))


def _dsa_attention(q, qi, wi, k, v, ki, bsz, seq, tq=128, tkb=512):
    t = bsz * seq
    tq = min(tq, seq)
    tkb = min(tkb, seq)
    nq = seq // tq
    topk = min(IDX_TOPK_MAX, seq // 4)
    idx_bits = max(seq - 1, 1).bit_length()
    qrow = lambda width: pl.BlockSpec((tq, width), lambda b, i: (b * nq + i, 0))
    kv = lambda width: pl.BlockSpec((seq, width), lambda b, i: (b, 0), pipeline_mode=pl.Buffered(1))
    return pl.pallas_call(
        functools.partial(_dsa_kernel, tq=tq, tkb=tkb, topk=topk, seq=seq, idx_bits=idx_bits,
                          scale=DSA_HEAD_DIM ** -0.5),
        grid=(bsz, nq),
        in_specs=[qrow(DSA_HEADS * DSA_HEAD_DIM), qrow(IDX_HEADS * IDX_DIM), qrow(LANES),
                  kv(DSA_HEAD_DIM), kv(LANES), kv(IDX_DIM)],
        out_specs=qrow(DSA_HEADS * DSA_HEAD_DIM),
        out_shape=jax.ShapeDtypeStruct((t, DSA_HEADS * DSA_HEAD_DIM), BF16),
        scratch_shapes=[
            pltpu.VMEM((DSA_HEADS * tq, DSA_HEAD_DIM), BF16),
            pltpu.VMEM((IDX_HEADS * tq, IDX_DIM), BF16),
            pltpu.VMEM((tq, seq), I32),
            pltpu.VMEM((tq, 1), I32),
            pltpu.VMEM((DSA_HEADS, tq, LANES), F32),
            pltpu.VMEM((DSA_HEADS, tq, tkb), F32),
            pltpu.VMEM((DSA_HEADS * tq, LANES), F32),
            pltpu.SMEM((1,), F32),
        ],
        compiler_params=_cparams(("parallel", "arbitrary")),
        name="dsa_attention",
    )(q, qi, wi, k, v, ki)


def _mla_prep_kernel(cq_ref, ckv_ref, kr_ref, cos_ref, sin_ref, qg_ref, kvg_ref, wq_ref, wk_ref, wv_ref,
                     q_ref, k_ref, v_ref):
    def rms(c, g):
        return c * lax.rsqrt(jnp.mean(c * c, axis=-1, keepdims=True) + RMS_EPS) * g

    cos = cos_ref[...]
    sin = sin_ref[...]
    qn = rms(cq_ref[...], qg_ref[...]).astype(BF16)
    kvn = rms(ckv_ref[...], kvg_ref[...]).astype(BF16)
    kr = kr_ref[...]
    for h in range(MLA_HEADS):
        hs = slice(h * LANES, (h + 1) * LANES)
        qh = jnp.dot(qn, wq_ref[:, hs], preferred_element_type=F32)
        q_ref[:, hs] = _rope128(qh, cos, sin, MLA_ROPE // 2).astype(q_ref.dtype)
        kh = jnp.dot(kvn, wk_ref[:, hs], preferred_element_type=F32)
        k_ref[:, hs] = (kh + kr).astype(k_ref.dtype)
        v_ref[:, hs] = jnp.dot(kvn, wv_ref[:, hs], preferred_element_type=F32).astype(v_ref.dtype)


def _mla_prep(c_q, c_kv, kr, cos, sin, q_norm_g, kv_norm_g, wq, wk, wv, tm=512):
    t = c_q.shape[0]
    tm = min(tm, t)
    row = lambda width: pl.BlockSpec((tm, width), lambda i: (i, 0))
    full = lambda a: pl.BlockSpec(a.shape, lambda i: (0, 0))
    qg = q_norm_g.reshape(1, -1)
    kvg = kv_norm_g.reshape(1, -1)
    width = MLA_HEADS * LANES
    return pl.pallas_call(
        _mla_prep_kernel,
        grid=(t // tm,),
        in_specs=[row(MLA_Q_RANK), row(MLA_KV_RANK), row(LANES), row(LANES), row(LANES),
                  full(qg), full(kvg), full(wq), full(wk), full(wv)],
        out_specs=[row(width)] * 3,
        out_shape=[jax.ShapeDtypeStruct((t, width), BF16)] * 3,
        compiler_params=_cparams(("parallel",)),
        name="mla_prep",
    )(c_q, c_kv, kr, cos, sin, qg, kvg, wq, wk, wv)


def _flash_kernel(q_ref, k_ref, v_ref, o_ref, *, tq, nh, scale):
    i = pl.program_id(2)
    row = lax.broadcasted_iota(I32, (tq, tq), 0)
    col = lax.broadcasted_iota(I32, (tq, tq), 1)
    causal = col <= row
    ones = jnp.ones((tq, LANES), BF16)

    def head_block(hh, start, carry, diag):
        m, acc = carry
        hs = slice(hh * LANES, (hh + 1) * LANES)
        kb = k_ref[pl.ds(start, tq), hs]
        vb = jnp.concatenate([v_ref[pl.ds(start, tq), hs], ones], axis=1)
        s = lax.dot_general(q_ref[:, hs], kb, _NT, preferred_element_type=F32) * scale
        if diag:
            s = jnp.where(causal, s, NEG_BIG)
        m_new = jnp.maximum(m, jnp.max(s, axis=-1, keepdims=True))
        alpha = jnp.exp(m - m_new)
        p = jnp.exp(s - m_new)
        acc = alpha * acc + jnp.dot(p.astype(BF16), vb, preferred_element_type=F32)
        return m_new, acc

    def block(j, carries, diag):
        start = pl.multiple_of(j * tq, tq)
        return tuple(head_block(hh, start, carries[hh], diag) for hh in range(nh))

    carries = tuple((jnp.full((tq, 1), NEG_BIG, F32), jnp.zeros((tq, 2 * LANES), F32)) for _ in range(nh))
    carries = block(i, carries, True)
    carries = lax.fori_loop(0, i, lambda jj, c: block(i - 1 - jj, c, False), carries)
    for hh in range(nh):
        acc = carries[hh][1]
        o_ref[:, hh * LANES:(hh + 1) * LANES] = (acc[:, :LANES] / acc[:, LANES:]).astype(o_ref.dtype)


def _causal_attention(q, k, v, bsz, seq, heads, scale, tq=512, nh=2):
    tq = min(tq, seq)
    nq = seq // tq
    t = bsz * seq
    wide = nh * LANES
    kv = pl.BlockSpec((seq, wide), lambda b, hh, i: (b, hh))
    qo = pl.BlockSpec((tq, wide), lambda b, hh, i: (b * nq + i, hh))
    return pl.pallas_call(
        functools.partial(_flash_kernel, tq=tq, nh=nh, scale=scale),
        grid=(bsz, heads // nh, nq),
        in_specs=[qo, kv, kv],
        out_specs=qo,
        out_shape=jax.ShapeDtypeStruct((t, heads * LANES), BF16),
        compiler_params=_cparams(("parallel", "parallel", "arbitrary")),
        name="mla_attention",
    )(q, k, v)


def _route(sb, s):
    per = N_EXPERTS // N_GROUPS
    gscore = []
    for g in range(N_GROUPS):
        a, b, c, d = sb[per * g:per * g + per]
        hi1, lo1 = jnp.maximum(a, b), jnp.minimum(a, b)
        hi2, lo2 = jnp.maximum(c, d), jnp.minimum(c, d)
        top1 = jnp.maximum(hi1, hi2)
        top2 = jnp.maximum(jnp.minimum(hi1, hi2), jnp.maximum(lo1, lo2))
        gscore.append(top1 + top2)
    best, gidx = gscore[0], jnp.zeros(gscore[0].shape, I32)
    for g in range(1, N_GROUPS):
        upd = gscore[g] > best
        best = jnp.where(upd, gscore[g], best)
        gidx = jnp.where(upd, g, gidx)

    def pick(rows, j):
        out = rows[j]
        for g in range(1, N_GROUPS):
            out = jnp.where(gidx == g, rows[per * g + j], out)
        return out

    vals = [pick(sb, j) for j in range(per)]
    raws = [pick(s, j) for j in range(per)]
    b1, i1, s1 = vals[0], jnp.zeros(gidx.shape, I32), raws[0]
    for j in range(1, per):
        upd = vals[j] > b1
        b1 = jnp.where(upd, vals[j], b1)
        i1 = jnp.where(upd, j, i1)
        s1 = jnp.where(upd, raws[j], s1)
    b2, i2, s2 = jnp.zeros_like(b1), jnp.full(gidx.shape, -1, I32), jnp.zeros_like(b1)
    for j in range(per):
        upd = (i1 != j) & ((i2 < 0) | (vals[j] > b2))
        b2 = jnp.where(upd, vals[j], b2)
        i2 = jnp.where(upd, j, i2)
        s2 = jnp.where(upd, raws[j], s2)
    denom = s1 + s2
    w1 = s1 / denom
    w2 = s2 / denom
    gates = []
    for e in range(N_EXPERTS):
        in_group = gidx == (e // per)
        gates.append(jnp.where(in_group & (i1 == e % per), w1,
                               jnp.where(in_group & (i2 == e % per), w2, 0.0)))
    return gates


def _post_kernel(x_ref, mix_ref, mq_ref, kvm_ref, wo_ref, g_ref, b_ref, rw_ref, rb_ref, x1_ref, gate_ref):
    mq = mq_ref[...]
    kvm = kvm_ref[0]
    mem = []
    for h in range(MEM_HEADS):
        hs = slice(h * MEM_HEAD_DIM, (h + 1) * MEM_HEAD_DIM)
        vs = slice(MEM_WIDTH + h * MEM_HEAD_DIM, MEM_WIDTH + (h + 1) * MEM_HEAD_DIM)
        s = lax.dot_general(mq[:, hs], kvm[:, hs], _NT, preferred_element_type=F32) * (MEM_HEAD_DIM ** -0.5)
        p = jnp.exp(s - jnp.max(s, axis=-1, keepdims=True))
        p = p / jnp.sum(p, axis=-1, keepdims=True)
        mem.append(jnp.dot(p.astype(BF16), kvm[:, vs], preferred_element_type=F32))
    y = jnp.dot(mix_ref[...], wo_ref[0:D_MODEL, :], preferred_element_type=F32)
    for h in range(MEM_HEADS):
        r0 = D_MODEL + h * MEM_HEAD_DIM
        y = y + jnp.dot(mem[h].astype(BF16), wo_ref[r0:r0 + MEM_HEAD_DIM, :], preferred_element_type=F32)
    x1 = _layer_norm(DEEPNORM_ALPHA * x_ref[...] + y, g_ref[...], b_ref[...])
    x1_ref[...] = x1
    logits = lax.dot_general(rw_ref[...], x1.astype(BF16), _NT, preferred_element_type=F32)
    s_all = _sigmoid(logits)
    sb_all = s_all + rb_ref[...]
    gates = _route([sb_all[e:e + 1, :] for e in range(N_EXPERTS)], [s_all[e:e + 1, :] for e in range(N_EXPERTS)])
    for e in range(N_EXPERTS):
        gate_ref[e:e + 1, :] = gates[e]


def _post_mixer(x, mix, mq, kvm, w_o, ln_g, ln_b, rw_t, rbias, seq, tm=256):
    t = x.shape[0]
    tm = min(tm, seq)
    per_seq = seq // tm
    row = lambda width: pl.BlockSpec((tm, width), lambda i: (i, 0))
    full = lambda a: pl.BlockSpec(a.shape, lambda i: (0,) * a.ndim)
    g = ln_g.reshape(1, -1)
    b = ln_b.reshape(1, -1)
    return pl.pallas_call(
        _post_kernel,
        grid=(t // tm,),
        in_specs=[row(D_MODEL), row(D_MODEL), row(MEM_WIDTH),
                  pl.BlockSpec((1,) + kvm.shape[1:], lambda i: (i // per_seq, 0, 0)),
                  full(w_o), full(g), full(b), full(rw_t), full(rbias)],
        out_specs=[row(D_MODEL), pl.BlockSpec((N_EXPERTS, tm), lambda i: (0, i))],
        out_shape=[jax.ShapeDtypeStruct((t, D_MODEL), F32), jax.ShapeDtypeStruct((N_EXPERTS, t), F32)],
        compiler_params=_cparams(("parallel",)),
        name="post_mixer",
    )(x, mix, mq, kvm, w_o, g, b, rw_t, rbias)


def _moe_kernel(x_ref, gate_ref, win_ref, wout_ref, g_ref, b_ref, o_ref, xb_ref, y_ref):
    e = pl.program_id(1)

    @pl.when(e == 0)
    def _():
        xb_ref[...] = x_ref[...].astype(BF16)
        y_ref[...] = jnp.zeros(y_ref.shape, F32)

    h = jnp.dot(xb_ref[...], win_ref[0], preferred_element_type=F32)
    gate_in = h[:, :EXPERT_FF]
    act = gate_in * _sigmoid(gate_in) * h[:, EXPERT_FF:]
    yb = jnp.dot(act.astype(BF16), wout_ref[0], preferred_element_type=F32)
    gates = gate_ref[...]
    lane = lax.broadcasted_iota(I32, gates.shape, 1)
    gcol = jnp.sum(jnp.where(lane == e, gates, 0.0), axis=-1, keepdims=True)
    y_ref[...] += yb * gcol

    @pl.when(e == N_EXPERTS - 1)
    def _():
        o_ref[...] = _layer_norm(DEEPNORM_ALPHA * x_ref[...] + y_ref[...], g_ref[...], b_ref[...])


def _moe(x1, gates, w_in, w_out, ln_g, ln_b, tm=512):
    t = x1.shape[0]
    tm = min(tm, t)
    g = ln_g.reshape(1, -1)
    b = ln_b.reshape(1, -1)
    vec = pl.BlockSpec((1, D_MODEL), lambda i, e: (0, 0))
    return pl.pallas_call(
        _moe_kernel,
        grid=(t // tm, N_EXPERTS),
        in_specs=[pl.BlockSpec((tm, D_MODEL), lambda i, e: (i, 0)),
                  pl.BlockSpec((tm, LANES), lambda i, e: (i, 0)),
                  pl.BlockSpec((1, D_MODEL, 2 * EXPERT_FF), lambda i, e: (e, 0, 0)),
                  pl.BlockSpec((1, EXPERT_FF, D_MODEL), lambda i, e: (e, 0, 0)),
                  vec, vec],
        out_specs=pl.BlockSpec((tm, D_MODEL), lambda i, e: (i, 0)),
        out_shape=jax.ShapeDtypeStruct((t, D_MODEL), F32),
        scratch_shapes=[pltpu.VMEM((tm, D_MODEL), BF16), pltpu.VMEM((tm, D_MODEL), F32)],
        compiler_params=_cparams(("parallel", "arbitrary")),
        name="moe_ffn",
    )(x1, gates, w_in, w_out, g, b)


def _pad_cols(w, width, offset=0):
    out = jnp.zeros((w.shape[0], width), w.dtype)
    return out.at[:, offset:offset + w.shape[1]].set(w)


def _rope_rows(half, lane_lo, lane_hi):
    inv_freq = ROPE_THETA ** (-jnp.arange(half, dtype=F32) / half)
    lane = jnp.arange(LANES)
    active = (lane >= lane_lo) & (lane < lane_hi)
    pos_in = (lane - lane_lo) % (2 * half)
    freq = jnp.where(active, inv_freq[pos_in % half], 0.0).astype(F32)
    sign = jnp.where(active, jnp.where(pos_in < half, -1.0, 1.0), 0.0).astype(F32)
    return freq.reshape(1, LANES), sign.reshape(1, LANES)


def kernel(x, mem, positions, router_w, router_bias, l0_w_in, l0_mem_wkv, l0_w_o, l0_ln_g, l0_ln_b, l0_moe_w_in, l0_moe_w_out, l1_w_in, l1_conv_w, l1_conv_b, l1_conv_norm_g, l1_conv_norm_b, l1_mem_wkv, l1_w_o, l1_ln_g, l1_ln_b, l1_moe_w_in, l1_moe_w_out, l2_w_in, l2_mem_wkv, l2_w_o, l2_ln_g, l2_ln_b, l2_moe_w_in, l2_moe_w_out, l3_w_in, l3_q_norm_g, l3_kv_norm_g, l3_w_q_up, l3_w_kv_up, l3_mem_wkv, l3_w_o, l3_ln_g, l3_ln_b, l3_moe_w_in, l3_moe_w_out):
    bsz, seq, d = x.shape
    t = bsz * seq
    n_mem = mem.shape[1]
    xt = x.reshape(t, d)
    mem2 = mem.reshape(bsz * n_mem, d)
    pos_b = jnp.broadcast_to(positions.astype(F32).reshape(t, 1), (t, LANES))
    rw_t = router_w.T.astype(BF16)
    rbias = router_bias.astype(F32).reshape(N_EXPERTS, 1)

    def finish_layer(xt, mix, mq, mem_wkv, w_o, ln_g, ln_b, moe_w_in, moe_w_out):
        (kvm,) = _proj(mem2, mem_wkv.astype(BF16), [(0, 2 * MEM_WIDTH, 2 * MEM_WIDTH, 0)], [BF16],
                       tm=n_mem, name="mem_kv")
        kvm = kvm.reshape(bsz, n_mem, 2 * MEM_WIDTH)
        x1, gates_t = _post_mixer(xt, mix, mq, kvm, w_o.astype(BF16), ln_g[0], ln_b[0], rw_t, rbias, seq)
        gates = _pad_cols(gates_t.T, LANES)
        return _moe(x1, gates, moe_w_in.astype(BF16), moe_w_out.astype(BF16), ln_g[1], ln_b[1])

    w0 = l0_w_in.astype(BF16)
    n0 = w0.shape[1]
    (h0,) = _proj(xt, w0, [(0, n0, n0, 0)], [BF16], name="proj0")
    mix = _sb_attention(h0, bsz, seq)
    mq = h0[:, 3 * D_MODEL:]
    xt = finish_layer(xt, mix, mq, l0_mem_wkv, l0_w_o, l0_ln_g, l0_ln_b, l0_moe_w_in, l0_moe_w_out)

    w1 = l1_w_in.astype(BF16)
    a, g, mq = _proj(xt, w1, [(0, D_MODEL, D_MODEL, 0), (D_MODEL, D_MODEL, D_MODEL, 0),
                              (2 * D_MODEL, MEM_WIDTH, MEM_WIDTH, 0)], [F32, F32, BF16], name="proj1")
    mix = _conformer_conv(a, g, l1_conv_w, l1_conv_b, l1_conv_norm_g, l1_conv_norm_b, seq)
    xt = finish_layer(xt, mix, mq, l1_mem_wkv, l1_w_o, l1_ln_g, l1_ln_b, l1_moe_w_in, l1_moe_w_out)

    w2 = l2_w_in
    c = 0
    parts = []
    for width in (DSA_HEADS * DSA_HEAD_DIM, DSA_HEAD_DIM, DSA_HEAD_DIM, IDX_HEADS * IDX_DIM, IDX_DIM, IDX_HEADS,
                  MEM_WIDTH):
        parts.append(w2[:, c:c + width])
        c += width
    wq, wk, wv, wqi, wki, wwi, wmq = parts
    w2p = jnp.concatenate([wq, _pad_cols(wk, LANES), _pad_cols(wv, LANES), wqi, _pad_cols(wki, LANES),
                           _pad_cols(wwi, LANES), wmq], axis=1).astype(BF16)
    freq, sign = _rope_rows(DSA_HEAD_DIM // 2, 0, LANES)
    cos_a, sin_a = _rope_tables(pos_b, freq, sign)
    rh = DSA_HEAD_DIM // 2
    groups = [(0, 1024, 1024, rh), (1024, LANES, DSA_HEAD_DIM, rh), (1152, LANES, LANES, 0),
              (1280, 512, 512, rh), (1792, LANES, IDX_DIM, rh), (1920, LANES, LANES, 0),
              (2048, MEM_WIDTH, MEM_WIDTH, 0)]
    q, k, v, qi, ki, wi, mq = _proj(xt, w2p, groups, [BF16, BF16, BF16, BF16, BF16, F32, BF16],
                                    rope=(cos_a, sin_a), name="proj2")
    mix = _dsa_attention(q, qi, wi, k, v, ki, bsz, seq)
    xt = finish_layer(xt, mix, mq, l2_mem_wkv, l2_w_o, l2_ln_g, l2_ln_b, l2_moe_w_in, l2_moe_w_out)

    w3 = l3_w_in
    rope_lo = MLA_NOPE
    w_cq = w3[:, :MLA_Q_RANK]
    w_ckv = w3[:, MLA_Q_RANK:MLA_Q_RANK + MLA_KV_RANK]
    w_kr = w3[:, MLA_Q_RANK + MLA_KV_RANK:MLA_Q_RANK + MLA_KV_RANK + MLA_ROPE]
    w_mq = w3[:, MLA_Q_RANK + MLA_KV_RANK + MLA_ROPE:]
    w3p = jnp.concatenate([w_cq, w_ckv, _pad_cols(w_kr, LANES, rope_lo), w_mq], axis=1).astype(BF16)
    freq, sign = _rope_rows(MLA_ROPE // 2, rope_lo, rope_lo + MLA_ROPE)
    cos_b, sin_b = _rope_tables(pos_b, freq, sign)
    groups = [(0, MLA_Q_RANK, MLA_Q_RANK, 0), (MLA_Q_RANK, MLA_KV_RANK, MLA_KV_RANK, 0),
              (MLA_Q_RANK + MLA_KV_RANK, LANES, LANES, MLA_ROPE // 2),
              (MLA_Q_RANK + MLA_KV_RANK + LANES, MEM_WIDTH, MEM_WIDTH, 0)]
    c_q, c_kv, kr, mq = _proj(xt, w3p, groups, [F32, F32, F32, BF16], rope=(cos_b, sin_b), name="proj3")
    qk_dim = MLA_NOPE + MLA_ROPE
    wq_heads = l3_w_q_up.reshape(MLA_Q_RANK, MLA_HEADS, qk_dim)
    wq_p = jnp.zeros((MLA_Q_RANK, MLA_HEADS, LANES), F32).at[:, :, :qk_dim].set(wq_heads)
    wkv_heads = l3_w_kv_up.reshape(MLA_KV_RANK, MLA_HEADS, MLA_NOPE + MLA_V)
    wk_p = jnp.zeros((MLA_KV_RANK, MLA_HEADS, LANES), F32).at[:, :, :MLA_NOPE].set(wkv_heads[:, :, :MLA_NOPE])
    wv_p = wkv_heads[:, :, MLA_NOPE:]
    flat = lambda w: w.reshape(w.shape[0], MLA_HEADS * LANES).astype(BF16)
    qh, kh, vh = _mla_prep(c_q, c_kv, kr, cos_b, sin_b, l3_q_norm_g, l3_kv_norm_g, flat(wq_p), flat(wk_p),
                           flat(wv_p))
    mix = _causal_attention(qh, kh, vh, bsz, seq, MLA_HEADS, qk_dim ** -0.5)
    xt = finish_layer(xt, mix, mq, l3_mem_wkv, l3_w_o, l3_ln_g, l3_ln_b, l3_moe_w_in, l3_moe_w_out)
    return xt.reshape(bsz, seq, d)
```

```python
import functools

import jax
import jax.numpy as jnp
from jax import lax
from jax.experimental import pallas as pl
from jax.experimental.pallas import tpu as pltpu

F32 = jnp.float32
BF16 = jnp.bfloat16
I32 = jnp.int32

D_MODEL = 1024
DEPTH = 4
ROPE_THETA = 10000.0
LN_EPS = 1e-5
RMS_EPS = 1e-6
DEEPNORM_ALPHA = (2.0 * DEPTH) ** 0.25
SB_HEADS = 8
CONV_WIDTH = 31
DSA_HEADS = 16
DSA_HEAD_DIM = 64
IDX_HEADS = 8
IDX_DIM = 64
IDX_TOPK_MAX = 256
MLA_HEADS = 8
MLA_Q_RANK = 384
MLA_KV_RANK = 256
MLA_NOPE = 64
MLA_ROPE = 32
MLA_V = 128
MEM_HEADS = 4
MEM_HEAD_DIM = 64
MEM_WIDTH = 256
N_EXPERTS = 16
N_GROUPS = 4
EXPERT_FF = 512
TOPK_EXPERTS = 2
MOE_CHUNK = 512

LANES = 128
VMEM_LIMIT_BYTES = 56 * 1024 * 1024
NEG_BIG = -1e30
MAX_SAFE_SHIFT = 40.0
INT_MIN = -2 ** 31

_NT = (((1,), (1,)), ((), ()))


def _cparams(sem):
    return pltpu.CompilerParams(dimension_semantics=sem, vmem_limit_bytes=VMEM_LIMIT_BYTES)


def _layer_norm(v, g, b):
    mu = jnp.mean(v, axis=-1, keepdims=True)
    d = v - mu
    var = jnp.mean(d * d, axis=-1, keepdims=True)
    return d * lax.rsqrt(var + LN_EPS) * g + b


def _sigmoid(v):
    return 1.0 / (1.0 + jnp.exp(-v))


def _rope_table_kernel(pos_ref, freq_ref, sign_ref, cos_ref, sin_ref):
    ang = pos_ref[...] * freq_ref[...]
    cos_ref[...] = jnp.cos(ang)
    sin_ref[...] = jnp.sin(ang) * sign_ref[...]


def _rope_tables(pos_b, freq_row, sign_row, tm=512):
    t = pos_b.shape[0]
    row = pl.BlockSpec((tm, LANES), lambda i: (i, 0))
    one = pl.BlockSpec((1, LANES), lambda i: (0, 0))
    return pl.pallas_call(
        _rope_table_kernel,
        grid=(t // tm,),
        in_specs=[row, one, one],
        out_specs=[row, row],
        out_shape=[jax.ShapeDtypeStruct((t, LANES), F32)] * 2,
        compiler_params=_cparams(("parallel",)),
        name="rope_tables",
    )(pos_b, freq_row, sign_row)


def _rope128(v, cos, sin, half):
    lane = lax.broadcasted_iota(I32, v.shape, 1)
    first = (lane & (2 * half - 1)) < half
    fwd = pltpu.roll(v, LANES - half, 1)
    bwd = pltpu.roll(v, half, 1)
    return v * cos + jnp.where(first, fwd, bwd) * sin


def _proj_kernel(*refs, groups, has_rope):
    x_ref, w_ref = refs[0], refs[1]
    nin = 4 if has_rope else 2
    outs = refs[nin:]
    x = x_ref[...].astype(BF16)
    for (start, width, out_width, rope_half), o_ref in zip(groups, outs):
        acc = jnp.dot(x, w_ref[:, start:start + width], preferred_element_type=F32)
        if rope_half:
            cos = refs[2][...]
            sin = refs[3][...]
            for c in range(width // LANES):
                r = _rope128(acc[:, c * LANES:(c + 1) * LANES], cos, sin, rope_half)
                lo = c * LANES
                hi = min(lo + LANES, out_width)
                o_ref[:, lo:hi] = r[:, :hi - lo].astype(o_ref.dtype)
        else:
            o_ref[...] = acc[:, :out_width].astype(o_ref.dtype)


def _proj(x, w, groups, out_dtypes, rope=None, tm=512, name="proj"):
    m, k = x.shape
    n = w.shape[1]
    tm = min(tm, m)
    in_specs = [pl.BlockSpec((tm, k), lambda i: (i, 0)), pl.BlockSpec((k, n), lambda i: (0, 0))]
    args = [x, w]
    if rope is not None:
        in_specs += [pl.BlockSpec((tm, LANES), lambda i: (i, 0))] * 2
        args += list(rope)
    out_specs = [pl.BlockSpec((tm, g[2]), lambda i: (i, 0)) for g in groups]
    out_shape = [jax.ShapeDtypeStruct((m, g[2]), dt) for g, dt in zip(groups, out_dtypes)]
    return pl.pallas_call(
        functools.partial(_proj_kernel, groups=tuple(groups), has_rope=rope is not None),
        grid=(m // tm,),
        in_specs=in_specs,
        out_specs=out_specs,
        out_shape=out_shape,
        compiler_params=_cparams(("parallel",)),
        name=name,
    )(*args)


def _sb_kernel(q_ref, k_ref, v_ref, o_ref, *, tq, ck, nh, scale):
    i = pl.program_id(2)
    r = lax.broadcasted_iota(I32, (ck, ck), 0)
    c = lax.broadcasted_iota(I32, (ck, ck), 1)
    suffix = (r >= c).astype(BF16)
    row = lax.broadcasted_iota(I32, (tq, ck), 0)
    col = lax.broadcasted_iota(I32, (tq, ck), 1)

    def head_block(hh, start, carry, diag):
        tail, acc = carry
        hs = slice(hh * LANES, (hh + 1) * LANES)
        kb = k_ref[pl.ds(start, tq), hs]
        z = lax.dot_general(q_ref[:, hs], kb, _NT, preferred_element_type=F32) * scale
        for cc in reversed(range(tq // ck)):
            zc = z[:, cc * ck:(cc + 1) * ck]
            sp = jnp.maximum(zc, 0.0) + jnp.log(1.0 + jnp.exp(-jnp.abs(zc)))
            if diag:
                strict = col + cc * ck < row
                sp = jnp.where(strict, sp, 0.0)
            cum = jnp.dot(sp.astype(BF16), suffix, preferred_element_type=F32)
            a = jnp.exp(zc - cum - tail)
            if diag:
                a = jnp.where(strict, a, 0.0)
            vb = v_ref[pl.ds(pl.multiple_of(start + cc * ck, ck), ck), hs]
            acc = acc + jnp.dot(a.astype(BF16), vb, preferred_element_type=F32)
            tail = tail + jnp.sum(sp, axis=-1, keepdims=True)
        return tail, acc

    def block(j, carries, diag):
        start = pl.multiple_of(j * tq, tq)
        return tuple(head_block(hh, start, carries[hh], diag) for hh in range(nh))

    carries = tuple((jnp.zeros((tq, 1), F32), jnp.zeros((tq, LANES), F32)) for _ in range(nh))
    carries = block(i, carries, True)
    carries = lax.fori_loop(0, i, lambda jj, c: block(i - 1 - jj, c, False), carries)
    for hh in range(nh):
        o_ref[:, hh * LANES:(hh + 1) * LANES] = carries[hh][1].astype(o_ref.dtype)


def _sb_attention(h, bsz, seq, tq=512, ck=256, nh=2):
    tq = min(tq, seq)
    ck = min(ck, tq)
    nq = seq // tq
    t = bsz * seq
    hg = SB_HEADS // nh
    wide = nh * LANES
    return pl.pallas_call(
        functools.partial(_sb_kernel, tq=tq, ck=ck, nh=nh, scale=LANES ** -0.5),
        grid=(bsz, hg, nq),
        in_specs=[
            pl.BlockSpec((tq, wide), lambda b, hh, i: (b * nq + i, hh)),
            pl.BlockSpec((seq, wide), lambda b, hh, i: (b, hg + hh)),
            pl.BlockSpec((seq, wide), lambda b, hh, i: (b, 2 * hg + hh)),
        ],
        out_specs=pl.BlockSpec((tq, wide), lambda b, hh, i: (b * nq + i, hh)),
        out_shape=jax.ShapeDtypeStruct((t, SB_HEADS * LANES), BF16),
        compiler_params=_cparams(("parallel", "parallel", "arbitrary")),
        name="sb_attention",
    )(h, h, h)


def _conv_kernel(a_ref, g_ref, ap_ref, gp_ref, w_ref, cb_ref, ng_ref, nb_ref, o_ref, hext_ref, y_ref,
                 *, tm, halo, tiles_per_seq):
    i = pl.program_id(0)
    hext_ref[halo:halo + tm, :] = a_ref[...] * _sigmoid(g_ref[...])
    prev = ap_ref[...] * _sigmoid(gp_ref[...])
    first = (i % tiles_per_seq) == 0
    hext_ref[0:halo, :] = jnp.where(first, 0.0, prev)
    off = halo - (CONV_WIDTH - 1)
    for c in range(D_MODEL // LANES):
        cs = slice(c * LANES, (c + 1) * LANES)
        acc = jnp.zeros((tm, LANES), F32)
        for j in range(CONV_WIDTH):
            acc = acc + w_ref[j:j + 1, cs] * hext_ref[off + j:off + j + tm, cs]
        y_ref[:, cs] = acc
    y = y_ref[...] + cb_ref[...]
    y = _layer_norm(y, ng_ref[...], nb_ref[...])
    o_ref[...] = (y * _sigmoid(y)).astype(o_ref.dtype)


def _conformer_conv(a, g, conv_w, conv_b, norm_g, norm_b, seq, tm=256, halo=32):
    t = a.shape[0]
    tm = min(tm, seq)
    r = tm // halo
    cur = pl.BlockSpec((tm, D_MODEL), lambda i: (i, 0))
    prv = pl.BlockSpec((halo, D_MODEL), lambda i: (jnp.maximum(i * r - 1, 0), 0))
    vec = pl.BlockSpec((1, D_MODEL), lambda i: (0, 0))
    return pl.pallas_call(
        functools.partial(_conv_kernel, tm=tm, halo=halo, tiles_per_seq=seq // tm),
        grid=(t // tm,),
        in_specs=[cur, cur, prv, prv, pl.BlockSpec((CONV_WIDTH, D_MODEL), lambda i: (0, 0)), vec, vec, vec],
        out_specs=cur,
        out_shape=jax.ShapeDtypeStruct((t, D_MODEL), BF16),
        scratch_shapes=[pltpu.VMEM((tm + halo, D_MODEL), F32), pltpu.VMEM((tm, D_MODEL), F32)],
        compiler_params=_cparams(("parallel",)),
        name="conformer_conv",
    )(a, g, a, g, conv_w, conv_b.reshape(1, -1), norm_g.reshape(1, -1), norm_b.reshape(1, -1))


def _dsa_kernel(q_ref, qi_ref, wi_ref, k_ref, v_ref, ki_ref, o_ref,
                qs_ref, qis_ref, keys_ref, eq_hi_ref, mrun_ref, mb_ref, acc_ref, kmax2_ref,
                *, tq, tkb, topk, seq, idx_bits, scale):
    i = pl.program_id(1)
    nkb = (i * tq + tq + tkb - 1) // tkb

    @pl.when(i == 0)
    def _():
        rows = min(tkb, seq)

        def body(c, m):
            kk = k_ref[pl.ds(pl.multiple_of(c * rows, rows), rows), :].astype(F32)
            return jnp.maximum(m, jnp.sum(kk * kk, axis=-1, keepdims=True))

        kmax2_ref[0] = jnp.max(lax.fori_loop(0, seq // rows, body, jnp.zeros((rows, 1), F32)))

    for h in range(DSA_HEADS):
        qs_ref[h * tq:(h + 1) * tq, :] = q_ref[:, h * DSA_HEAD_DIM:(h + 1) * DSA_HEAD_DIM]
    for h in range(IDX_HEADS):
        qis_ref[h * tq:(h + 1) * tq, :] = qi_ref[:, h * IDX_DIM:(h + 1) * IDX_DIM]
    wi = wi_ref[...] * (IDX_HEADS ** -0.5 * IDX_DIM ** -0.5)
    wcols = [wi[:, h:h + 1] for h in range(IDX_HEADS)]
    q_pos = i * tq + lax.broadcasted_iota(I32, (tq, 1), 0)
    col0 = lax.broadcasted_iota(I32, (tq, tkb), 1)

    def scores(jb, _):
        start = pl.multiple_of(jb * tkb, tkb)
        kib = ki_ref[pl.ds(start, tkb), :]
        dots = lax.dot_general(qis_ref[...], kib, _NT, preferred_element_type=F32)
        d3 = dots.reshape(IDX_HEADS, tq, tkb)
        sc = wcols[0] * jnp.maximum(d3[0], 0.0)
        for h in range(1, IDX_HEADS):
            sc = sc + wcols[h] * jnp.maximum(d3[h], 0.0)
        sc = sc + 0.0
        bits = lax.bitcast_convert_type(sc, I32)
        key = bits ^ ((bits >> 31) & 0x7FFFFFFF)
        key = jnp.where(col0 + start <= q_pos, key, INT_MIN)
        keys_ref[:, pl.ds(start, tkb)] = key
        return 0

    lax.fori_loop(0, nkb, scores, 0)

    def count_if(pred):
        def body(c, acc):
            start = pl.multiple_of(c * tkb, tkb)
            ind = jnp.where(pred(keys_ref[:, pl.ds(start, tkb)], start), 1.0, 0.0)
            for s in range(tkb // LANES):
                acc = acc + ind[:, s * LANES:(s + 1) * LANES]
            return acc
        acc = lax.fori_loop(0, nkb, body, jnp.zeros((tq, LANES), F32))
        return jnp.sum(acc, axis=-1, keepdims=True)

    kf = float(topk)
    tau = jnp.where(count_if(lambda kb, st: kb >= 0) >= kf, 0, INT_MIN).astype(I32)

    def bisect(it, tau):
        cand = tau + jnp.left_shift(jnp.int32(1), 30 - it)
        return jnp.where(count_if(lambda kb, st: kb >= cand) >= kf, cand, tau)

    tau = lax.fori_loop(0, 31, bisect, tau)
    short = tau == INT_MIN
    n_gt = count_if(lambda kb, st: kb > tau)
    n_ge = count_if(lambda kb, st: kb >= tau)
    need = kf - n_gt
    eq_hi_ref[...] = jnp.where(short, -1, seq).astype(I32)
    tied = jnp.max(jnp.where((n_ge > kf) & jnp.logical_not(short), 1.0, 0.0)) > 0.0

    @pl.when(tied)
    def _():
        def step(it, p):
            cand = p + jnp.left_shift(jnp.int32(1), idx_bits - it)
            cnt = count_if(lambda kb, st: (kb == tau) & (col0 + st < cand))
            return jnp.where((cand <= seq) & (cnt < need), cand, p)

        p = lax.fori_loop(0, idx_bits + 1, step, jnp.zeros((tq, 1), I32))
        eq_hi_ref[...] = jnp.where(short, -1, p)

    eq_hi = eq_hi_ref[...]

    def selected(start):
        kb = keys_ref[:, pl.ds(start, tkb)]
        return (kb > tau) | ((kb == tau) & (col0 + start <= eq_hi))

    qf = qs_ref[...].astype(F32)
    q_norm2 = jnp.sum(qf * qf, axis=-1, keepdims=True).reshape(DSA_HEADS, tq, 1)
    bound = jnp.sqrt(q_norm2 * kmax2_ref[0]) * scale
    bound_ok = jnp.max(bound) < MAX_SAFE_SHIFT

    @pl.when(bound_ok)
    def _():
        mb_ref[...] = jnp.broadcast_to(bound, mb_ref.shape)

    @pl.when(jnp.logical_not(bound_ok))
    def _():
        mrun_ref[...] = jnp.full(mrun_ref.shape, NEG_BIG, F32)

        def row_max(jb, _):
            start = pl.multiple_of(jb * tkb, tkb)
            s = lax.dot_general(qs_ref[...], k_ref[pl.ds(start, tkb), :], _NT, preferred_element_type=F32)
            s3 = jnp.where(selected(start)[None], s.reshape(DSA_HEADS, tq, tkb), NEG_BIG)
            m = mrun_ref[...]
            for c in range(tkb // LANES):
                m = jnp.maximum(m, s3[:, :, c * LANES:(c + 1) * LANES])
            mrun_ref[...] = m
            return 0

        lax.fori_loop(0, nkb, row_max, 0)
        m_row = jnp.max(mrun_ref[...], axis=-1, keepdims=True) * scale
        mb_ref[...] = jnp.broadcast_to(m_row, mb_ref.shape)

    acc_ref[...] = jnp.zeros(acc_ref.shape, F32)
    vlane = lax.broadcasted_iota(I32, (tkb, LANES), 1)

    def attend(jb, _):
        start = pl.multiple_of(jb * tkb, tkb)
        kb = k_ref[pl.ds(start, tkb), :]
        vb = jnp.where(vlane < DSA_HEAD_DIM, v_ref[pl.ds(start, tkb), :], 1.0).astype(BF16)
        s = lax.dot_general(qs_ref[...], kb, _NT, preferred_element_type=F32)
        t3 = s.reshape(DSA_HEADS, tq, tkb) * scale - mb_ref[...]
        p = jnp.where(selected(start)[None], jnp.exp(t3), 0.0)
        acc_ref[...] += jnp.dot(p.reshape(DSA_HEADS * tq, tkb).astype(BF16), vb, preferred_element_type=F32)
        return 0

    lax.fori_loop(0, nkb, attend, 0)
    acc = acc_ref[...]
    out = acc / pltpu.roll(acc, LANES - DSA_HEAD_DIM, 1)
    for h in range(DSA_HEADS):
        o_ref[:, h * DSA_HEAD_DIM:(h + 1) * DSA_HEAD_DIM] = (
            out[h * tq:(h + 1) * tq, :DSA_HEAD_DIM].astype(o_ref.dtype))


def _dsa_attention(q, qi, wi, k, v, ki, bsz, seq, tq=128, tkb=512):
    t = bsz * seq
    tq = min(tq, seq)
    tkb = min(tkb, seq)
    nq = seq // tq
    topk = min(IDX_TOPK_MAX, seq // 4)
    idx_bits = max(seq - 1, 1).bit_length()
    qrow = lambda width: pl.BlockSpec((tq, width), lambda b, i: (b * nq + i, 0))
    kv = lambda width: pl.BlockSpec((seq, width), lambda b, i: (b, 0), pipeline_mode=pl.Buffered(1))
    return pl.pallas_call(
        functools.partial(_dsa_kernel, tq=tq, tkb=tkb, topk=topk, seq=seq, idx_bits=idx_bits,
                          scale=DSA_HEAD_DIM ** -0.5),
        grid=(bsz, nq),
        in_specs=[qrow(DSA_HEADS * DSA_HEAD_DIM), qrow(IDX_HEADS * IDX_DIM), qrow(LANES),
                  kv(DSA_HEAD_DIM), kv(LANES), kv(IDX_DIM)],
        out_specs=qrow(DSA_HEADS * DSA_HEAD_DIM),
        out_shape=jax.ShapeDtypeStruct((t, DSA_HEADS * DSA_HEAD_DIM), BF16),
        scratch_shapes=[
            pltpu.VMEM((DSA_HEADS * tq, DSA_HEAD_DIM), BF16),
            pltpu.VMEM((IDX_HEADS * tq, IDX_DIM), BF16),
            pltpu.VMEM((tq, seq), I32),
            pltpu.VMEM((tq, 1), I32),
            pltpu.VMEM((DSA_HEADS, tq, LANES), F32),
            pltpu.VMEM((DSA_HEADS, tq, tkb), F32),
            pltpu.VMEM((DSA_HEADS * tq, LANES), F32),
            pltpu.SMEM((1,), F32),
        ],
        compiler_params=_cparams(("parallel", "arbitrary")),
        name="dsa_attention",
    )(q, qi, wi, k, v, ki)


def _mla_prep_kernel(cq_ref, ckv_ref, kr_ref, cos_ref, sin_ref, qg_ref, kvg_ref, wq_ref, wk_ref, wv_ref,
                     q_ref, k_ref, v_ref):
    def rms(c, g):
        return c * lax.rsqrt(jnp.mean(c * c, axis=-1, keepdims=True) + RMS_EPS) * g

    cos = cos_ref[...]
    sin = sin_ref[...]
    qn = rms(cq_ref[...], qg_ref[...]).astype(BF16)
    kvn = rms(ckv_ref[...], kvg_ref[...]).astype(BF16)
    kr = kr_ref[...]
    for h in range(MLA_HEADS):
        hs = slice(h * LANES, (h + 1) * LANES)
        qh = jnp.dot(qn, wq_ref[:, hs], preferred_element_type=F32)
        q_ref[:, hs] = _rope128(qh, cos, sin, MLA_ROPE // 2).astype(q_ref.dtype)
        kh = jnp.dot(kvn, wk_ref[:, hs], preferred_element_type=F32)
        k_ref[:, hs] = (kh + kr).astype(k_ref.dtype)
        v_ref[:, hs] = jnp.dot(kvn, wv_ref[:, hs], preferred_element_type=F32).astype(v_ref.dtype)


def _mla_prep(c_q, c_kv, kr, cos, sin, q_norm_g, kv_norm_g, wq, wk, wv, tm=512):
    t = c_q.shape[0]
    tm = min(tm, t)
    row = lambda width: pl.BlockSpec((tm, width), lambda i: (i, 0))
    full = lambda a: pl.BlockSpec(a.shape, lambda i: (0, 0))
    qg = q_norm_g.reshape(1, -1)
    kvg = kv_norm_g.reshape(1, -1)
    width = MLA_HEADS * LANES
    return pl.pallas_call(
        _mla_prep_kernel,
        grid=(t // tm,),
        in_specs=[row(MLA_Q_RANK), row(MLA_KV_RANK), row(LANES), row(LANES), row(LANES),
                  full(qg), full(kvg), full(wq), full(wk), full(wv)],
        out_specs=[row(width)] * 3,
        out_shape=[jax.ShapeDtypeStruct((t, width), BF16)] * 3,
        compiler_params=_cparams(("parallel",)),
        name="mla_prep",
    )(c_q, c_kv, kr, cos, sin, qg, kvg, wq, wk, wv)


def _flash_kernel(q_ref, k_ref, v_ref, o_ref, *, tq, nh, scale):
    i = pl.program_id(2)
    row = lax.broadcasted_iota(I32, (tq, tq), 0)
    col = lax.broadcasted_iota(I32, (tq, tq), 1)
    causal = col <= row
    ones = jnp.ones((tq, LANES), BF16)

    def head_block(hh, start, carry, diag):
        m, acc = carry
        hs = slice(hh * LANES, (hh + 1) * LANES)
        kb = k_ref[pl.ds(start, tq), hs]
        vb = jnp.concatenate([v_ref[pl.ds(start, tq), hs], ones], axis=1)
        s = lax.dot_general(q_ref[:, hs], kb, _NT, preferred_element_type=F32) * scale
        if diag:
            s = jnp.where(causal, s, NEG_BIG)
        m_new = jnp.maximum(m, jnp.max(s, axis=-1, keepdims=True))
        alpha = jnp.exp(m - m_new)
        p = jnp.exp(s - m_new)
        acc = alpha * acc + jnp.dot(p.astype(BF16), vb, preferred_element_type=F32)
        return m_new, acc

    def block(j, carries, diag):
        start = pl.multiple_of(j * tq, tq)
        return tuple(head_block(hh, start, carries[hh], diag) for hh in range(nh))

    carries = tuple((jnp.full((tq, 1), NEG_BIG, F32), jnp.zeros((tq, 2 * LANES), F32)) for _ in range(nh))
    carries = block(i, carries, True)
    carries = lax.fori_loop(0, i, lambda jj, c: block(i - 1 - jj, c, False), carries)
    for hh in range(nh):
        acc = carries[hh][1]
        o_ref[:, hh * LANES:(hh + 1) * LANES] = (acc[:, :LANES] / acc[:, LANES:]).astype(o_ref.dtype)


def _causal_attention(q, k, v, bsz, seq, heads, scale, tq=512, nh=2):
    tq = min(tq, seq)
    nq = seq // tq
    t = bsz * seq
    wide = nh * LANES
    kv = pl.BlockSpec((seq, wide), lambda b, hh, i: (b, hh))
    qo = pl.BlockSpec((tq, wide), lambda b, hh, i: (b * nq + i, hh))
    return pl.pallas_call(
        functools.partial(_flash_kernel, tq=tq, nh=nh, scale=scale),
        grid=(bsz, heads // nh, nq),
        in_specs=[qo, kv, kv],
        out_specs=qo,
        out_shape=jax.ShapeDtypeStruct((t, heads * LANES), BF16),
        compiler_params=_cparams(("parallel", "parallel", "arbitrary")),
        name="mla_attention",
    )(q, k, v)


def _route(sb, s):
    per = N_EXPERTS // N_GROUPS
    gscore = []
    for g in range(N_GROUPS):
        a, b, c, d = sb[per * g:per * g + per]
        hi1, lo1 = jnp.maximum(a, b), jnp.minimum(a, b)
        hi2, lo2 = jnp.maximum(c, d), jnp.minimum(c, d)
        top1 = jnp.maximum(hi1, hi2)
        top2 = jnp.maximum(jnp.minimum(hi1, hi2), jnp.maximum(lo1, lo2))
        gscore.append(top1 + top2)
    best, gidx = gscore[0], jnp.zeros(gscore[0].shape, I32)
    for g in range(1, N_GROUPS):
        upd = gscore[g] > best
        best = jnp.where(upd, gscore[g], best)
        gidx = jnp.where(upd, g, gidx)

    def pick(rows, j):
        out = rows[j]
        for g in range(1, N_GROUPS):
            out = jnp.where(gidx == g, rows[per * g + j], out)
        return out

    vals = [pick(sb, j) for j in range(per)]
    raws = [pick(s, j) for j in range(per)]
    b1, i1, s1 = vals[0], jnp.zeros(gidx.shape, I32), raws[0]
    for j in range(1, per):
        upd = vals[j] > b1
        b1 = jnp.where(upd, vals[j], b1)
        i1 = jnp.where(upd, j, i1)
        s1 = jnp.where(upd, raws[j], s1)
    b2, i2, s2 = jnp.zeros_like(b1), jnp.full(gidx.shape, -1, I32), jnp.zeros_like(b1)
    for j in range(per):
        upd = (i1 != j) & ((i2 < 0) | (vals[j] > b2))
        b2 = jnp.where(upd, vals[j], b2)
        i2 = jnp.where(upd, j, i2)
        s2 = jnp.where(upd, raws[j], s2)
    denom = s1 + s2
    return per * gidx + i1, per * gidx + i2, s1 / denom, s2 / denom


def _post_kernel(x_ref, mix_ref, mq_ref, kvm_ref, wo_ref, g_ref, b_ref, rw_ref, rb_ref,
                 x1_ref, esel_ref, gate_ref):
    mq = mq_ref[...]
    kvm = kvm_ref[0]
    mem = []
    for h in range(MEM_HEADS):
        hs = slice(h * MEM_HEAD_DIM, (h + 1) * MEM_HEAD_DIM)
        vs = slice(MEM_WIDTH + h * MEM_HEAD_DIM, MEM_WIDTH + (h + 1) * MEM_HEAD_DIM)
        s = lax.dot_general(mq[:, hs], kvm[:, hs], _NT, preferred_element_type=F32) * (MEM_HEAD_DIM ** -0.5)
        p = jnp.exp(s - jnp.max(s, axis=-1, keepdims=True))
        p = p / jnp.sum(p, axis=-1, keepdims=True)
        mem.append(jnp.dot(p.astype(BF16), kvm[:, vs], preferred_element_type=F32))
    y = jnp.dot(mix_ref[...], wo_ref[0:D_MODEL, :], preferred_element_type=F32)
    for h in range(MEM_HEADS):
        r0 = D_MODEL + h * MEM_HEAD_DIM
        y = y + jnp.dot(mem[h].astype(BF16), wo_ref[r0:r0 + MEM_HEAD_DIM, :], preferred_element_type=F32)
    x1 = _layer_norm(DEEPNORM_ALPHA * x_ref[...] + y, g_ref[...], b_ref[...])
    x1_ref[...] = x1
    logits = lax.dot_general(rw_ref[...], x1.astype(BF16), _NT, preferred_element_type=F32)
    s_all = _sigmoid(logits)
    sb_all = s_all + rb_ref[...]
    e1, e2, w1, w2 = _route([sb_all[e:e + 1, :] for e in range(N_EXPERTS)],
                            [s_all[e:e + 1, :] for e in range(N_EXPERTS)])
    esel_ref[0:1, :] = e1
    esel_ref[1:2, :] = e2
    gate_ref[0:1, :] = w1
    gate_ref[1:2, :] = w2


def _post_mixer(x, mix, mq, kvm, w_o, ln_g, ln_b, rw_t, rbias, seq, tm=256):
    t = x.shape[0]
    tm = min(tm, seq)
    per_seq = seq // tm
    row = lambda width: pl.BlockSpec((tm, width), lambda i: (i, 0))
    full = lambda a: pl.BlockSpec(a.shape, lambda i: (0,) * a.ndim)
    g = ln_g.reshape(1, -1)
    b = ln_b.reshape(1, -1)
    return pl.pallas_call(
        _post_kernel,
        grid=(t // tm,),
        in_specs=[row(D_MODEL), row(D_MODEL), row(MEM_WIDTH),
                  pl.BlockSpec((1,) + kvm.shape[1:], lambda i: (i // per_seq, 0, 0)),
                  full(w_o), full(g), full(b), full(rw_t), full(rbias)],
        out_specs=[row(D_MODEL), pl.BlockSpec((TOPK_EXPERTS, tm), lambda i: (0, i)),
                   pl.BlockSpec((TOPK_EXPERTS, tm), lambda i: (0, i))],
        out_shape=[jax.ShapeDtypeStruct((t, D_MODEL), F32), jax.ShapeDtypeStruct((TOPK_EXPERTS, t), I32),
                   jax.ShapeDtypeStruct((TOPK_EXPERTS, t), F32)],
        compiler_params=_cparams(("parallel",)),
        name="post_mixer",
    )(x, mix, mq, kvm, w_o, g, b, rw_t, rbias)


def _moe_rank_kernel(e_ref, rank_ref, cnt_ref, base_ref, *, tr):
    i = pl.program_id(0)

    @pl.when(i == 0)
    def _():
        base_ref[...] = jnp.zeros(base_ref.shape, F32)

    e1 = e_ref[0:1, :]
    e2 = e_ref[1:2, :]
    row = lax.broadcasted_iota(I32, (N_EXPERTS, tr), 0)
    onehot = jnp.where((row == e1) | (row == e2), 1.0, 0.0)
    before = (lax.broadcasted_iota(I32, (tr, tr), 0) < lax.broadcasted_iota(I32, (tr, tr), 1)).astype(BF16)
    seen = jnp.dot(onehot.astype(BF16), before, preferred_element_type=F32) + base_ref[:, 0:1]
    rank_ref[0:1, :] = jnp.sum(jnp.where(row == e1, seen, 0.0), axis=0, keepdims=True).astype(I32)
    rank_ref[1:2, :] = jnp.sum(jnp.where(row == e2, seen, 0.0), axis=0, keepdims=True).astype(I32)
    base_ref[...] = base_ref[...] + jnp.sum(onehot, axis=1, keepdims=True)
    cnt_ref[...] = base_ref[...]


def _moe_rank(esel, tr=512):
    t = esel.shape[1]
    tr = min(tr, t)
    return pl.pallas_call(
        functools.partial(_moe_rank_kernel, tr=tr),
        grid=(t // tr,),
        in_specs=[pl.BlockSpec((TOPK_EXPERTS, tr), lambda i: (0, i))],
        out_specs=[pl.BlockSpec((TOPK_EXPERTS, tr), lambda i: (0, i)),
                   pl.BlockSpec((N_EXPERTS, LANES), lambda i: (0, 0))],
        out_shape=[jax.ShapeDtypeStruct((TOPK_EXPERTS, t), I32), jax.ShapeDtypeStruct((N_EXPERTS, LANES), F32)],
        scratch_shapes=[pltpu.VMEM((N_EXPERTS, LANES), F32)],
        compiler_params=_cparams(("arbitrary",)),
        name="moe_rank",
    )(esel)


def _moe_scatter_kernel(zero_ref, slot_ref, x_ref, buf_ref, zeros_ref, sem, *, ts, n_chunks):
    i = pl.program_id(0)

    def row_copy(tok, k):
        return pltpu.make_async_copy(x_ref.at[pl.ds(tok, 1)], buf_ref.at[pl.ds(slot_ref[0, k, tok], 1)], sem)

    @pl.when(i == 0)
    def _():
        zeros_ref[...] = jnp.zeros(zeros_ref.shape, F32)

        def fill(c, _):
            @pl.when(zero_ref[c] != 0)
            def _():
                cp = pltpu.make_async_copy(
                    zeros_ref, buf_ref.at[pl.ds(pl.multiple_of(c * MOE_CHUNK, MOE_CHUNK), MOE_CHUNK)], sem)
                cp.start()
                cp.wait()
            return 0

        lax.fori_loop(0, n_chunks, fill, 0)

    def start(tok, _):
        for k in range(TOPK_EXPERTS):
            row_copy(tok, k).start()
        return 0

    def wait(tok, _):
        for k in range(TOPK_EXPERTS):
            row_copy(tok, k).wait()
        return 0

    lax.fori_loop(0, ts, start, 0, unroll=8)
    lax.fori_loop(0, ts, wait, 0, unroll=8)


def _moe_scatter(x1, slots, zero_flags, n_chunks, ts=512):
    t = x1.shape[0]
    grid_spec = pltpu.PrefetchScalarGridSpec(
        num_scalar_prefetch=1,
        grid=(t // ts,),
        in_specs=[pl.BlockSpec((1, TOPK_EXPERTS, ts), lambda i, z: (i, 0, 0), memory_space=pltpu.SMEM),
                  pl.BlockSpec((ts, D_MODEL), lambda i, z: (i, 0))],
        out_specs=pl.BlockSpec(memory_space=pl.ANY),
        scratch_shapes=[pltpu.VMEM((MOE_CHUNK, D_MODEL), F32), pltpu.SemaphoreType.DMA(())],
    )
    return pl.pallas_call(
        functools.partial(_moe_scatter_kernel, ts=ts, n_chunks=n_chunks),
        grid_spec=grid_spec,
        out_shape=jax.ShapeDtypeStruct((n_chunks * MOE_CHUNK, D_MODEL), F32),
        compiler_params=_cparams(("arbitrary",)),
        name="moe_scatter",
    )(zero_flags, slots, x1)


def _moe_ffn_kernel(cexp_ref, nused_ref, x_ref, win_ref, wout_ref, y_ref):
    c = pl.program_id(0)

    @pl.when(c < nused_ref[0])
    def _():
        h = jnp.dot(x_ref[...].astype(BF16), win_ref[0], preferred_element_type=F32)
        gate_in = h[:, :EXPERT_FF]
        act = gate_in * _sigmoid(gate_in) * h[:, EXPERT_FF:]
        y_ref[...] = jnp.dot(act.astype(BF16), wout_ref[0], preferred_element_type=F32)

    @pl.when(c >= nused_ref[0])
    def _():
        y_ref[...] = jnp.zeros(y_ref.shape, F32)


def _moe_ffn(buf, chunk_exp, n_used, w_in, w_out):
    n_chunks = chunk_exp.shape[0]
    grid_spec = pltpu.PrefetchScalarGridSpec(
        num_scalar_prefetch=2,
        grid=(n_chunks,),
        in_specs=[pl.BlockSpec((MOE_CHUNK, D_MODEL), lambda c, ce, nu: (c, 0)),
                  pl.BlockSpec((1, D_MODEL, 2 * EXPERT_FF), lambda c, ce, nu: (ce[c], 0, 0)),
                  pl.BlockSpec((1, EXPERT_FF, D_MODEL), lambda c, ce, nu: (ce[c], 0, 0))],
        out_specs=pl.BlockSpec((MOE_CHUNK, D_MODEL), lambda c, ce, nu: (c, 0)),
    )
    return pl.pallas_call(
        _moe_ffn_kernel,
        grid_spec=grid_spec,
        out_shape=jax.ShapeDtypeStruct(buf.shape, F32),
        compiler_params=_cparams(("arbitrary",)),
        name="moe_ffn",
    )(chunk_exp, n_used, buf, w_in, w_out)


def _moe_combine_kernel(slot_ref, x_ref, gate_ref, g_ref, b_ref, yb_ref, o_ref, rows_ref, sem, *, tc):
    def row_copy(tok, k):
        return pltpu.make_async_copy(yb_ref.at[pl.ds(slot_ref[0, k, tok], 1)], rows_ref.at[k, pl.ds(tok, 1)], sem)

    def start(tok, _):
        for k in range(TOPK_EXPERTS):
            row_copy(tok, k).start()
        return 0

    def wait(tok, _):
        for k in range(TOPK_EXPERTS):
            row_copy(tok, k).wait()
        return 0

    lax.fori_loop(0, tc, start, 0, unroll=8)
    lax.fori_loop(0, tc, wait, 0, unroll=8)
    gates = gate_ref[...]
    y = rows_ref[0] * gates[:, 0:1] + rows_ref[1] * gates[:, 1:2]
    o_ref[...] = _layer_norm(DEEPNORM_ALPHA * x_ref[...] + y, g_ref[...], b_ref[...])


def _moe_combine(x1, gates, slots, yb, ln_g, ln_b, tc=512):
    t = x1.shape[0]
    g = ln_g.reshape(1, -1)
    b = ln_b.reshape(1, -1)
    vec = pl.BlockSpec((1, D_MODEL), lambda i: (0, 0))
    return pl.pallas_call(
        functools.partial(_moe_combine_kernel, tc=tc),
        grid=(t // tc,),
        in_specs=[pl.BlockSpec((1, TOPK_EXPERTS, tc), lambda i: (i, 0, 0), memory_space=pltpu.SMEM),
                  pl.BlockSpec((tc, D_MODEL), lambda i: (i, 0)),
                  pl.BlockSpec((tc, LANES), lambda i: (i, 0)),
                  vec, vec,
                  pl.BlockSpec(memory_space=pl.ANY)],
        out_specs=pl.BlockSpec((tc, D_MODEL), lambda i: (i, 0)),
        out_shape=jax.ShapeDtypeStruct((t, D_MODEL), F32),
        scratch_shapes=[pltpu.VMEM((TOPK_EXPERTS, tc, D_MODEL), F32), pltpu.SemaphoreType.DMA(())],
        compiler_params=_cparams(("arbitrary",)),
        name="moe_combine",
    )(slots, x1, gates, g, b, yb)


def _moe(x1, esel, gate_t, w_in, w_out, ln_g, ln_b, tile=512):
    t = x1.shape[0]
    tile = min(tile, t)
    n_chunks = -(-t * TOPK_EXPERTS // MOE_CHUNK) + N_EXPERTS
    rank, cnt = _moe_rank(esel)
    counts = cnt[:, 0].astype(I32)
    padded = (counts + MOE_CHUNK - 1) // MOE_CHUNK * MOE_CHUNK
    pad_end = jnp.cumsum(padded)
    pad_start = pad_end - padded
    slot = pad_start[esel] + rank
    slots = slot.reshape(TOPK_EXPERTS, t // tile, tile).transpose(1, 0, 2)
    chunk_lo = jnp.arange(n_chunks, dtype=I32) * MOE_CHUNK
    chunk_exp = jnp.clip(jnp.searchsorted(pad_end, chunk_lo, side='right'), 0, N_EXPERTS - 1).astype(I32)
    n_used = (pad_end[-1] // MOE_CHUNK).astype(I32).reshape(1)
    is_last = (chunk_lo + MOE_CHUNK) == pad_end[chunk_exp]
    zero_flags = (is_last | (chunk_lo >= pad_end[-1])).astype(I32)
    buf = _moe_scatter(x1, slots, zero_flags, n_chunks, ts=tile)
    yb = _moe_ffn(buf, chunk_exp, n_used, w_in, w_out)
    gates = _pad_cols(gate_t.T, LANES)
    return _moe_combine(x1, gates, slots, yb, ln_g, ln_b, tc=tile)


def _pad_cols(w, width, offset=0):
    out = jnp.zeros((w.shape[0], width), w.dtype)
    return out.at[:, offset:offset + w.shape[1]].set(w)


def _rope_rows(half, lane_lo, lane_hi):
    inv_freq = ROPE_THETA ** (-jnp.arange(half, dtype=F32) / half)
    lane = jnp.arange(LANES)
    active = (lane >= lane_lo) & (lane < lane_hi)
    pos_in = (lane - lane_lo) % (2 * half)
    freq = jnp.where(active, inv_freq[pos_in % half], 0.0).astype(F32)
    sign = jnp.where(active, jnp.where(pos_in < half, -1.0, 1.0), 0.0).astype(F32)
    return freq.reshape(1, LANES), sign.reshape(1, LANES)


def kernel(x, mem, positions, router_w, router_bias, l0_w_in, l0_mem_wkv, l0_w_o, l0_ln_g, l0_ln_b, l0_moe_w_in, l0_moe_w_out, l1_w_in, l1_conv_w, l1_conv_b, l1_conv_norm_g, l1_conv_norm_b, l1_mem_wkv, l1_w_o, l1_ln_g, l1_ln_b, l1_moe_w_in, l1_moe_w_out, l2_w_in, l2_mem_wkv, l2_w_o, l2_ln_g, l2_ln_b, l2_moe_w_in, l2_moe_w_out, l3_w_in, l3_q_norm_g, l3_kv_norm_g, l3_w_q_up, l3_w_kv_up, l3_mem_wkv, l3_w_o, l3_ln_g, l3_ln_b, l3_moe_w_in, l3_moe_w_out):
    bsz, seq, d = x.shape
    t = bsz * seq
    n_mem = mem.shape[1]
    xt = x.reshape(t, d)
    mem2 = mem.reshape(bsz * n_mem, d)
    pos_b = jnp.broadcast_to(positions.astype(F32).reshape(t, 1), (t, LANES))
    rw_t = router_w.T.astype(BF16)
    rbias = router_bias.astype(F32).reshape(N_EXPERTS, 1)

    def finish_layer(xt, mix, mq, mem_wkv, w_o, ln_g, ln_b, moe_w_in, moe_w_out):
        (kvm,) = _proj(mem2, mem_wkv.astype(BF16), [(0, 2 * MEM_WIDTH, 2 * MEM_WIDTH, 0)], [BF16],
                       tm=n_mem, name="mem_kv")
        kvm = kvm.reshape(bsz, n_mem, 2 * MEM_WIDTH)
        x1, esel, gate_t = _post_mixer(xt, mix, mq, kvm, w_o.astype(BF16), ln_g[0], ln_b[0], rw_t, rbias, seq)
        return _moe(x1, esel, gate_t, moe_w_in.astype(BF16), moe_w_out.astype(BF16), ln_g[1], ln_b[1])

    w0 = l0_w_in.astype(BF16)
    n0 = w0.shape[1]
    (h0,) = _proj(xt, w0, [(0, n0, n0, 0)], [BF16], name="proj0")
    mix = _sb_attention(h0, bsz, seq)
    mq = h0[:, 3 * D_MODEL:]
    xt = finish_layer(xt, mix, mq, l0_mem_wkv, l0_w_o, l0_ln_g, l0_ln_b, l0_moe_w_in, l0_moe_w_out)

    w1 = l1_w_in.astype(BF16)
    a, g, mq = _proj(xt, w1, [(0, D_MODEL, D_MODEL, 0), (D_MODEL, D_MODEL, D_MODEL, 0),
                              (2 * D_MODEL, MEM_WIDTH, MEM_WIDTH, 0)], [F32, F32, BF16], name="proj1")
    mix = _conformer_conv(a, g, l1_conv_w, l1_conv_b, l1_conv_norm_g, l1_conv_norm_b, seq)
    xt = finish_layer(xt, mix, mq, l1_mem_wkv, l1_w_o, l1_ln_g, l1_ln_b, l1_moe_w_in, l1_moe_w_out)

    w2 = l2_w_in
    c = 0
    parts = []
    for width in (DSA_HEADS * DSA_HEAD_DIM, DSA_HEAD_DIM, DSA_HEAD_DIM, IDX_HEADS * IDX_DIM, IDX_DIM, IDX_HEADS,
                  MEM_WIDTH):
        parts.append(w2[:, c:c + width])
        c += width
    wq, wk, wv, wqi, wki, wwi, wmq = parts
    w2p = jnp.concatenate([wq, _pad_cols(wk, LANES), _pad_cols(wv, LANES), wqi, _pad_cols(wki, LANES),
                           _pad_cols(wwi, LANES), wmq], axis=1).astype(BF16)
    freq, sign = _rope_rows(DSA_HEAD_DIM // 2, 0, LANES)
    cos_a, sin_a = _rope_tables(pos_b, freq, sign)
    rh = DSA_HEAD_DIM // 2
    groups = [(0, 1024, 1024, rh), (1024, LANES, DSA_HEAD_DIM, rh), (1152, LANES, LANES, 0),
              (1280, 512, 512, rh), (1792, LANES, IDX_DIM, rh), (1920, LANES, LANES, 0),
              (2048, MEM_WIDTH, MEM_WIDTH, 0)]
    q, k, v, qi, ki, wi, mq = _proj(xt, w2p, groups, [BF16, BF16, BF16, BF16, BF16, F32, BF16],
                                    rope=(cos_a, sin_a), name="proj2")
    mix = _dsa_attention(q, qi, wi, k, v, ki, bsz, seq)
    xt = finish_layer(xt, mix, mq, l2_mem_wkv, l2_w_o, l2_ln_g, l2_ln_b, l2_moe_w_in, l2_moe_w_out)

    w3 = l3_w_in
    rope_lo = MLA_NOPE
    w_cq = w3[:, :MLA_Q_RANK]
    w_ckv = w3[:, MLA_Q_RANK:MLA_Q_RANK + MLA_KV_RANK]
    w_kr = w3[:, MLA_Q_RANK + MLA_KV_RANK:MLA_Q_RANK + MLA_KV_RANK + MLA_ROPE]
    w_mq = w3[:, MLA_Q_RANK + MLA_KV_RANK + MLA_ROPE:]
    w3p = jnp.concatenate([w_cq, w_ckv, _pad_cols(w_kr, LANES, rope_lo), w_mq], axis=1).astype(BF16)
    freq, sign = _rope_rows(MLA_ROPE // 2, rope_lo, rope_lo + MLA_ROPE)
    cos_b, sin_b = _rope_tables(pos_b, freq, sign)
    groups = [(0, MLA_Q_RANK, MLA_Q_RANK, 0), (MLA_Q_RANK, MLA_KV_RANK, MLA_KV_RANK, 0),
              (MLA_Q_RANK + MLA_KV_RANK, LANES, LANES, MLA_ROPE // 2),
              (MLA_Q_RANK + MLA_KV_RANK + LANES, MEM_WIDTH, MEM_WIDTH, 0)]
    c_q, c_kv, kr, mq = _proj(xt, w3p, groups, [F32, F32, F32, BF16], rope=(cos_b, sin_b), name="proj3")
    qk_dim = MLA_NOPE + MLA_ROPE
    wq_heads = l3_w_q_up.reshape(MLA_Q_RANK, MLA_HEADS, qk_dim)
    wq_p = jnp.zeros((MLA_Q_RANK, MLA_HEADS, LANES), F32).at[:, :, :qk_dim].set(wq_heads)
    wkv_heads = l3_w_kv_up.reshape(MLA_KV_RANK, MLA_HEADS, MLA_NOPE + MLA_V)
    wk_p = jnp.zeros((MLA_KV_RANK, MLA_HEADS, LANES), F32).at[:, :, :MLA_NOPE].set(wkv_heads[:, :, :MLA_NOPE])
    wv_p = wkv_heads[:, :, MLA_NOPE:]
    flat = lambda w: w.reshape(w.shape[0], MLA_HEADS * LANES).astype(BF16)
    qh, kh, vh = _mla_prep(c_q, c_kv, kr, cos_b, sin_b, l3_q_norm_g, l3_kv_norm_g, flat(wq_p), flat(wk_p),
                           flat(wv_p))
    mix = _causal_attention(qh, kh, vh, bsz, seq, MLA_HEADS, qk_dim ** -0.5)
    xt = finish_layer(xt, mix, mq, l3_mem_wkv, l3_w_o, l3_ln_g, l3_ln_b, l3_moe_w_in, l3_moe_w_out)
    return xt.reshape(bsz, seq, d)
```

```python
import functools

import jax
import jax.numpy as jnp
from jax import lax
from jax.experimental import pallas as pl
from jax.experimental.pallas import tpu as pltpu

F32 = jnp.float32
BF16 = jnp.bfloat16
I32 = jnp.int32

D_MODEL = 1024
DEPTH = 4
ROPE_THETA = 10000.0
LN_EPS = 1e-5
RMS_EPS = 1e-6
DEEPNORM_ALPHA = (2.0 * DEPTH) ** 0.25
SB_HEADS = 8
CONV_WIDTH = 31
DSA_HEADS = 16
DSA_HEAD_DIM = 64
IDX_HEADS = 8
IDX_DIM = 64
IDX_TOPK_MAX = 256
MLA_HEADS = 8
MLA_Q_RANK = 384
MLA_KV_RANK = 256
MLA_NOPE = 64
MLA_ROPE = 32
MLA_V = 128
MEM_HEADS = 4
MEM_HEAD_DIM = 64
MEM_WIDTH = 256
N_EXPERTS = 16
N_GROUPS = 4
EXPERT_FF = 512
TOPK_EXPERTS = 2
MOE_CHUNK = 512

LANES = 128
VMEM_LIMIT_BYTES = 56 * 1024 * 1024
NEG_BIG = -1e30
MAX_SAFE_SHIFT = 40.0
INT_MIN = -2 ** 31

_NT = (((1,), (1,)), ((), ()))


def _cparams(sem):
    return pltpu.CompilerParams(dimension_semantics=sem, vmem_limit_bytes=VMEM_LIMIT_BYTES)


def _layer_norm(v, g, b):
    mu = jnp.mean(v, axis=-1, keepdims=True)
    d = v - mu
    var = jnp.mean(d * d, axis=-1, keepdims=True)
    return d * lax.rsqrt(var + LN_EPS) * g + b


def _sigmoid(v):
    return 1.0 / (1.0 + jnp.exp(-v))


def _rope_table_kernel(pos_ref, freq_ref, sign_ref, cos_ref, sin_ref):
    ang = pos_ref[...] * freq_ref[...]
    cos_ref[...] = jnp.cos(ang)
    sin_ref[...] = jnp.sin(ang) * sign_ref[...]


def _rope_tables(pos_b, freq_row, sign_row, tm=512):
    t = pos_b.shape[0]
    row = pl.BlockSpec((tm, LANES), lambda i: (i, 0))
    one = pl.BlockSpec((1, LANES), lambda i: (0, 0))
    return pl.pallas_call(
        _rope_table_kernel,
        grid=(t // tm,),
        in_specs=[row, one, one],
        out_specs=[row, row],
        out_shape=[jax.ShapeDtypeStruct((t, LANES), F32)] * 2,
        compiler_params=_cparams(("parallel",)),
        name="rope_tables",
    )(pos_b, freq_row, sign_row)


def _rope128(v, cos, sin, half):
    lane = lax.broadcasted_iota(I32, v.shape, 1)
    first = (lane & (2 * half - 1)) < half
    fwd = pltpu.roll(v, LANES - half, 1)
    bwd = pltpu.roll(v, half, 1)
    return v * cos + jnp.where(first, fwd, bwd) * sin


def _proj_kernel(*refs, groups, has_rope):
    x_ref, w_ref = refs[0], refs[1]
    nin = 4 if has_rope else 2
    outs = refs[nin:]
    x = x_ref[...].astype(BF16)
    for (start, width, out_width, rope_half), o_ref in zip(groups, outs):
        acc = jnp.dot(x, w_ref[:, start:start + width], preferred_element_type=F32)
        if rope_half:
            cos = refs[2][...]
            sin = refs[3][...]
            for c in range(width // LANES):
                r = _rope128(acc[:, c * LANES:(c + 1) * LANES], cos, sin, rope_half)
                lo = c * LANES
                hi = min(lo + LANES, out_width)
                o_ref[:, lo:hi] = r[:, :hi - lo].astype(o_ref.dtype)
        else:
            o_ref[...] = acc[:, :out_width].astype(o_ref.dtype)


def _proj(x, w, groups, out_dtypes, rope=None, tm=512, name="proj"):
    m, k = x.shape
    n = w.shape[1]
    tm = min(tm, m)
    in_specs = [pl.BlockSpec((tm, k), lambda i: (i, 0)), pl.BlockSpec((k, n), lambda i: (0, 0))]
    args = [x, w]
    if rope is not None:
        in_specs += [pl.BlockSpec((tm, LANES), lambda i: (i, 0))] * 2
        args += list(rope)
    out_specs = [pl.BlockSpec((tm, g[2]), lambda i: (i, 0)) for g in groups]
    out_shape = [jax.ShapeDtypeStruct((m, g[2]), dt) for g, dt in zip(groups, out_dtypes)]
    return pl.pallas_call(
        functools.partial(_proj_kernel, groups=tuple(groups), has_rope=rope is not None),
        grid=(m // tm,),
        in_specs=in_specs,
        out_specs=out_specs,
        out_shape=out_shape,
        compiler_params=_cparams(("parallel",)),
        name=name,
    )(*args)


def _sb_kernel(q_ref, k_ref, v_ref, o_ref, *, tq, ck, nh, scale):
    i = pl.program_id(2)
    r = lax.broadcasted_iota(I32, (ck, ck), 0)
    c = lax.broadcasted_iota(I32, (ck, ck), 1)
    suffix = (r >= c).astype(BF16)
    row = lax.broadcasted_iota(I32, (tq, ck), 0)
    col = lax.broadcasted_iota(I32, (tq, ck), 1)

    def head_block(hh, start, carry, diag):
        tail, acc = carry
        hs = slice(hh * LANES, (hh + 1) * LANES)
        kb = k_ref[pl.ds(start, tq), hs]
        z = lax.dot_general(q_ref[:, hs], kb, _NT, preferred_element_type=F32) * scale
        for cc in reversed(range(tq // ck)):
            zc = z[:, cc * ck:(cc + 1) * ck]
            sp = jnp.maximum(zc, 0.0) + jnp.log(1.0 + jnp.exp(-jnp.abs(zc)))
            if diag:
                strict = col + cc * ck < row
                sp = jnp.where(strict, sp, 0.0)
            cum = jnp.dot(sp.astype(BF16), suffix, preferred_element_type=F32)
            a = jnp.exp(zc - cum - tail)
            if diag:
                a = jnp.where(strict, a, 0.0)
            vb = v_ref[pl.ds(pl.multiple_of(start + cc * ck, ck), ck), hs]
            acc = acc + jnp.dot(a.astype(BF16), vb, preferred_element_type=F32)
            tail = tail + jnp.sum(sp, axis=-1, keepdims=True)
        return tail, acc

    def block(j, carries, diag):
        start = pl.multiple_of(j * tq, tq)
        return tuple(head_block(hh, start, carries[hh], diag) for hh in range(nh))

    carries = tuple((jnp.zeros((tq, 1), F32), jnp.zeros((tq, LANES), F32)) for _ in range(nh))
    carries = block(i, carries, True)
    carries = lax.fori_loop(0, i, lambda jj, c: block(i - 1 - jj, c, False), carries)
    for hh in range(nh):
        o_ref[:, hh * LANES:(hh + 1) * LANES] = carries[hh][1].astype(o_ref.dtype)


def _sb_attention(h, bsz, seq, tq=512, ck=256, nh=2):
    tq = min(tq, seq)
    ck = min(ck, tq)
    nq = seq // tq
    t = bsz * seq
    hg = SB_HEADS // nh
    wide = nh * LANES
    return pl.pallas_call(
        functools.partial(_sb_kernel, tq=tq, ck=ck, nh=nh, scale=LANES ** -0.5),
        grid=(bsz, hg, nq),
        in_specs=[
            pl.BlockSpec((tq, wide), lambda b, hh, i: (b * nq + i, hh)),
            pl.BlockSpec((seq, wide), lambda b, hh, i: (b, hg + hh)),
            pl.BlockSpec((seq, wide), lambda b, hh, i: (b, 2 * hg + hh)),
        ],
        out_specs=pl.BlockSpec((tq, wide), lambda b, hh, i: (b * nq + i, hh)),
        out_shape=jax.ShapeDtypeStruct((t, SB_HEADS * LANES), BF16),
        compiler_params=_cparams(("parallel", "parallel", "arbitrary")),
        name="sb_attention",
    )(h, h, h)


def _conv_kernel(a_ref, g_ref, ap_ref, gp_ref, w_ref, cb_ref, ng_ref, nb_ref, o_ref, hext_ref, y_ref,
                 *, tm, halo, tiles_per_seq):
    i = pl.program_id(0)
    hext_ref[halo:halo + tm, :] = a_ref[...] * _sigmoid(g_ref[...])
    prev = ap_ref[...] * _sigmoid(gp_ref[...])
    first = (i % tiles_per_seq) == 0
    hext_ref[0:halo, :] = jnp.where(first, 0.0, prev)
    off = halo - (CONV_WIDTH - 1)
    for c in range(D_MODEL // LANES):
        cs = slice(c * LANES, (c + 1) * LANES)
        acc = jnp.zeros((tm, LANES), F32)
        for j in range(CONV_WIDTH):
            acc = acc + w_ref[j:j + 1, cs] * hext_ref[off + j:off + j + tm, cs]
        y_ref[:, cs] = acc
    y = y_ref[...] + cb_ref[...]
    y = _layer_norm(y, ng_ref[...], nb_ref[...])
    o_ref[...] = (y * _sigmoid(y)).astype(o_ref.dtype)


def _conformer_conv(a, g, conv_w, conv_b, norm_g, norm_b, seq, tm=256, halo=32):
    t = a.shape[0]
    tm = min(tm, seq)
    r = tm // halo
    cur = pl.BlockSpec((tm, D_MODEL), lambda i: (i, 0))
    prv = pl.BlockSpec((halo, D_MODEL), lambda i: (jnp.maximum(i * r - 1, 0), 0))
    vec = pl.BlockSpec((1, D_MODEL), lambda i: (0, 0))
    return pl.pallas_call(
        functools.partial(_conv_kernel, tm=tm, halo=halo, tiles_per_seq=seq // tm),
        grid=(t // tm,),
        in_specs=[cur, cur, prv, prv, pl.BlockSpec((CONV_WIDTH, D_MODEL), lambda i: (0, 0)), vec, vec, vec],
        out_specs=cur,
        out_shape=jax.ShapeDtypeStruct((t, D_MODEL), BF16),
        scratch_shapes=[pltpu.VMEM((tm + halo, D_MODEL), F32), pltpu.VMEM((tm, D_MODEL), F32)],
        compiler_params=_cparams(("parallel",)),
        name="conformer_conv",
    )(a, g, a, g, conv_w, conv_b.reshape(1, -1), norm_g.reshape(1, -1), norm_b.reshape(1, -1))


def _dsa_kernel(q_ref, qi_ref, wi_ref, k_ref, v_ref, ki_ref, o_ref,
                qs_ref, qis_ref, keys_ref, eq_hi_ref, mrun_ref, mb_ref, acc_ref, kmax2_ref,
                *, tq, tkb, topk, seq, idx_bits, scale):
    i = pl.program_id(1)
    nkb = (i * tq + tq + tkb - 1) // tkb

    @pl.when(i == 0)
    def _():
        rows = min(tkb, seq)

        def body(c, m):
            kk = k_ref[pl.ds(pl.multiple_of(c * rows, rows), rows), :].astype(F32)
            return jnp.maximum(m, jnp.sum(kk * kk, axis=-1, keepdims=True))

        kmax2_ref[0] = jnp.max(lax.fori_loop(0, seq // rows, body, jnp.zeros((rows, 1), F32)))

    for h in range(DSA_HEADS):
        qs_ref[h * tq:(h + 1) * tq, :] = q_ref[:, h * DSA_HEAD_DIM:(h + 1) * DSA_HEAD_DIM]
    for h in range(IDX_HEADS):
        qis_ref[h * tq:(h + 1) * tq, :] = qi_ref[:, h * IDX_DIM:(h + 1) * IDX_DIM]
    wi = wi_ref[...] * (IDX_HEADS ** -0.5 * IDX_DIM ** -0.5)
    wcols = [wi[:, h:h + 1] for h in range(IDX_HEADS)]
    q_pos = i * tq + lax.broadcasted_iota(I32, (tq, 1), 0)
    col0 = lax.broadcasted_iota(I32, (tq, tkb), 1)

    def scores(jb, _):
        start = pl.multiple_of(jb * tkb, tkb)
        kib = ki_ref[pl.ds(start, tkb), :]
        dots = lax.dot_general(qis_ref[...], kib, _NT, preferred_element_type=F32)
        d3 = dots.reshape(IDX_HEADS, tq, tkb)
        sc = wcols[0] * jnp.maximum(d3[0], 0.0)
        for h in range(1, IDX_HEADS):
            sc = sc + wcols[h] * jnp.maximum(d3[h], 0.0)
        sc = sc + 0.0
        bits = lax.bitcast_convert_type(sc, I32)
        key = bits ^ ((bits >> 31) & 0x7FFFFFFF)
        key = jnp.where(col0 + start <= q_pos, key, INT_MIN)
        keys_ref[:, pl.ds(start, tkb)] = key
        return 0

    lax.fori_loop(0, nkb, scores, 0)

    def count_if(pred):
        def body(c, acc):
            start = pl.multiple_of(c * tkb, tkb)
            ind = jnp.where(pred(keys_ref[:, pl.ds(start, tkb)], start), 1.0, 0.0)
            for s in range(tkb // LANES):
                acc = acc + ind[:, s * LANES:(s + 1) * LANES]
            return acc
        acc = lax.fori_loop(0, nkb, body, jnp.zeros((tq, LANES), F32))
        return jnp.sum(acc, axis=-1, keepdims=True)

    kf = float(topk)
    short = count_if(lambda kb, st: kb > INT_MIN) < kf
    n_pos = count_if(lambda kb, st: kb >= 0)
    tau = jnp.where(n_pos >= kf, 0, INT_MIN).astype(I32)
    n_ge = jnp.where(n_pos >= kf, n_pos, 2.0 * seq)

    def unsettled(n_ge):
        return jnp.max(jnp.where(short | (n_ge == kf), 0.0, 1.0))

    def bisect(state):
        it, tau, n_ge, _ = state
        cand = tau + jnp.left_shift(jnp.int32(1), 30 - it)
        n_cand = count_if(lambda kb, st: kb >= cand)
        keep = n_cand >= kf
        tau = jnp.where(keep, cand, tau)
        n_ge = jnp.where(keep, n_cand, n_ge)
        return it + 1, tau, n_ge, unsettled(n_ge)

    _, tau, n_ge, _ = lax.while_loop(lambda st: (st[0] < 31) & (st[3] > 0.0), bisect,
                                     (jnp.int32(0), tau, n_ge, unsettled(n_ge)))
    n_gt = count_if(lambda kb, st: kb > tau)
    need = kf - n_gt
    eq_hi_ref[...] = jnp.where(short, -1, seq).astype(I32)
    tied = jnp.max(jnp.where((n_ge > kf) & jnp.logical_not(short), 1.0, 0.0)) > 0.0

    @pl.when(tied)
    def _():
        def step(it, p):
            cand = p + jnp.left_shift(jnp.int32(1), idx_bits - it)
            cnt = count_if(lambda kb, st: (kb == tau) & (col0 + st < cand))
            return jnp.where((cand <= seq) & (cnt < need), cand, p)

        p = lax.fori_loop(0, idx_bits + 1, step, jnp.zeros((tq, 1), I32))
        eq_hi_ref[...] = jnp.where(short, -1, p)

    eq_hi = eq_hi_ref[...]

    def selected(start):
        kb = keys_ref[:, pl.ds(start, tkb)]
        return (kb > tau) | ((kb == tau) & (col0 + start <= eq_hi))

    qf = qs_ref[...].astype(F32)
    q_norm2 = jnp.sum(qf * qf, axis=-1, keepdims=True).reshape(DSA_HEADS, tq, 1)
    bound = jnp.sqrt(q_norm2 * kmax2_ref[0]) * scale
    bound_ok = jnp.max(bound) < MAX_SAFE_SHIFT

    @pl.when(bound_ok)
    def _():
        mb_ref[...] = jnp.broadcast_to(bound, mb_ref.shape)

    @pl.when(jnp.logical_not(bound_ok))
    def _():
        mrun_ref[...] = jnp.full(mrun_ref.shape, NEG_BIG, F32)

        def row_max(jb, _):
            start = pl.multiple_of(jb * tkb, tkb)
            s = lax.dot_general(qs_ref[...], k_ref[pl.ds(start, tkb), :], _NT, preferred_element_type=F32)
            s3 = jnp.where(selected(start)[None], s.reshape(DSA_HEADS, tq, tkb), NEG_BIG)
            m = mrun_ref[...]
            for c in range(tkb // LANES):
                m = jnp.maximum(m, s3[:, :, c * LANES:(c + 1) * LANES])
            mrun_ref[...] = m
            return 0

        lax.fori_loop(0, nkb, row_max, 0)
        m_row = jnp.max(mrun_ref[...], axis=-1, keepdims=True) * scale
        mb_ref[...] = jnp.broadcast_to(m_row, mb_ref.shape)

    acc_ref[...] = jnp.zeros(acc_ref.shape, F32)
    vlane = lax.broadcasted_iota(I32, (tkb, LANES), 1)

    def attend(jb, _):
        start = pl.multiple_of(jb * tkb, tkb)
        kb = k_ref[pl.ds(start, tkb), :]
        vb = jnp.where(vlane < DSA_HEAD_DIM, v_ref[pl.ds(start, tkb), :], 1.0).astype(BF16)
        s = lax.dot_general(qs_ref[...], kb, _NT, preferred_element_type=F32)
        t3 = s.reshape(DSA_HEADS, tq, tkb) * scale - mb_ref[...]
        p = jnp.where(selected(start)[None], jnp.exp(t3), 0.0)
        acc_ref[...] += jnp.dot(p.reshape(DSA_HEADS * tq, tkb).astype(BF16), vb, preferred_element_type=F32)
        return 0

    lax.fori_loop(0, nkb, attend, 0)
    acc = acc_ref[...]
    out = acc / pltpu.roll(acc, LANES - DSA_HEAD_DIM, 1)
    for h in range(DSA_HEADS):
        o_ref[:, h * DSA_HEAD_DIM:(h + 1) * DSA_HEAD_DIM] = (
            out[h * tq:(h + 1) * tq, :DSA_HEAD_DIM].astype(o_ref.dtype))


def _dsa_attention(q, qi, wi, k, v, ki, bsz, seq, tq=128, tkb=512):
    t = bsz * seq
    tq = min(tq, seq)
    tkb = min(tkb, seq)
    nq = seq // tq
    topk = min(IDX_TOPK_MAX, seq // 4)
    idx_bits = max(seq - 1, 1).bit_length()
    qrow = lambda width: pl.BlockSpec((tq, width), lambda b, i: (b * nq + i, 0))
    kv = lambda width: pl.BlockSpec((seq, width), lambda b, i: (b, 0), pipeline_mode=pl.Buffered(1))
    return pl.pallas_call(
        functools.partial(_dsa_kernel, tq=tq, tkb=tkb, topk=topk, seq=seq, idx_bits=idx_bits,
                          scale=DSA_HEAD_DIM ** -0.5),
        grid=(bsz, nq),
        in_specs=[qrow(DSA_HEADS * DSA_HEAD_DIM), qrow(IDX_HEADS * IDX_DIM), qrow(LANES),
                  kv(DSA_HEAD_DIM), kv(LANES), kv(IDX_DIM)],
        out_specs=qrow(DSA_HEADS * DSA_HEAD_DIM),
        out_shape=jax.ShapeDtypeStruct((t, DSA_HEADS * DSA_HEAD_DIM), BF16),
        scratch_shapes=[
            pltpu.VMEM((DSA_HEADS * tq, DSA_HEAD_DIM), BF16),
            pltpu.VMEM((IDX_HEADS * tq, IDX_DIM), BF16),
            pltpu.VMEM((tq, seq), I32),
            pltpu.VMEM((tq, 1), I32),
            pltpu.VMEM((DSA_HEADS, tq, LANES), F32),
            pltpu.VMEM((DSA_HEADS, tq, tkb), F32),
            pltpu.VMEM((DSA_HEADS * tq, LANES), F32),
            pltpu.SMEM((1,), F32),
        ],
        compiler_params=_cparams(("parallel", "arbitrary")),
        name="dsa_attention",
    )(q, qi, wi, k, v, ki)


def _mla_prep_kernel(cq_ref, ckv_ref, kr_ref, cos_ref, sin_ref, qg_ref, kvg_ref, wq_ref, wk_ref, wv_ref,
                     q_ref, k_ref, v_ref):
    def rms(c, g):
        return c * lax.rsqrt(jnp.mean(c * c, axis=-1, keepdims=True) + RMS_EPS) * g

    cos = cos_ref[...]
    sin = sin_ref[...]
    qn = rms(cq_ref[...], qg_ref[...]).astype(BF16)
    kvn = rms(ckv_ref[...], kvg_ref[...]).astype(BF16)
    kr = kr_ref[...]
    for h in range(MLA_HEADS):
        hs = slice(h * LANES, (h + 1) * LANES)
        qh = jnp.dot(qn, wq_ref[:, hs], preferred_element_type=F32)
        q_ref[:, hs] = _rope128(qh, cos, sin, MLA_ROPE // 2).astype(q_ref.dtype)
        kh = jnp.dot(kvn, wk_ref[:, hs], preferred_element_type=F32)
        k_ref[:, hs] = (kh + kr).astype(k_ref.dtype)
        v_ref[:, hs] = jnp.dot(kvn, wv_ref[:, hs], preferred_element_type=F32).astype(v_ref.dtype)


def _mla_prep(c_q, c_kv, kr, cos, sin, q_norm_g, kv_norm_g, wq, wk, wv, tm=512):
    t = c_q.shape[0]
    tm = min(tm, t)
    row = lambda width: pl.BlockSpec((tm, width), lambda i: (i, 0))
    full = lambda a: pl.BlockSpec(a.shape, lambda i: (0, 0))
    qg = q_norm_g.reshape(1, -1)
    kvg = kv_norm_g.reshape(1, -1)
    width = MLA_HEADS * LANES
    return pl.pallas_call(
        _mla_prep_kernel,
        grid=(t // tm,),
        in_specs=[row(MLA_Q_RANK), row(MLA_KV_RANK), row(LANES), row(LANES), row(LANES),
                  full(qg), full(kvg), full(wq), full(wk), full(wv)],
        out_specs=[row(width)] * 3,
        out_shape=[jax.ShapeDtypeStruct((t, width), BF16)] * 3,
        compiler_params=_cparams(("parallel",)),
        name="mla_prep",
    )(c_q, c_kv, kr, cos, sin, qg, kvg, wq, wk, wv)


def _flash_kernel(q_ref, k_ref, v_ref, o_ref, *, tq, nh, scale):
    i = pl.program_id(2)
    row = lax.broadcasted_iota(I32, (tq, tq), 0)
    col = lax.broadcasted_iota(I32, (tq, tq), 1)
    causal = col <= row
    ones = jnp.ones((tq, LANES), BF16)

    def head_block(hh, start, carry, diag):
        m, acc = carry
        hs = slice(hh * LANES, (hh + 1) * LANES)
        kb = k_ref[pl.ds(start, tq), hs]
        vb = jnp.concatenate([v_ref[pl.ds(start, tq), hs], ones], axis=1)
        s = lax.dot_general(q_ref[:, hs], kb, _NT, preferred_element_type=F32) * scale
        if diag:
            s = jnp.where(causal, s, NEG_BIG)
        m_new = jnp.maximum(m, jnp.max(s, axis=-1, keepdims=True))
        alpha = jnp.exp(m - m_new)
        p = jnp.exp(s - m_new)
        acc = alpha * acc + jnp.dot(p.astype(BF16), vb, preferred_element_type=F32)
        return m_new, acc

    def block(j, carries, diag):
        start = pl.multiple_of(j * tq, tq)
        return tuple(head_block(hh, start, carries[hh], diag) for hh in range(nh))

    carries = tuple((jnp.full((tq, 1), NEG_BIG, F32), jnp.zeros((tq, 2 * LANES), F32)) for _ in range(nh))
    carries = block(i, carries, True)
    carries = lax.fori_loop(0, i, lambda jj, c: block(i - 1 - jj, c, False), carries)
    for hh in range(nh):
        acc = carries[hh][1]
        o_ref[:, hh * LANES:(hh + 1) * LANES] = (acc[:, :LANES] / acc[:, LANES:]).astype(o_ref.dtype)


def _causal_attention(q, k, v, bsz, seq, heads, scale, tq=512, nh=2):
    tq = min(tq, seq)
    nq = seq // tq
    t = bsz * seq
    wide = nh * LANES
    kv = pl.BlockSpec((seq, wide), lambda b, hh, i: (b, hh))
    qo = pl.BlockSpec((tq, wide), lambda b, hh, i: (b * nq + i, hh))
    return pl.pallas_call(
        functools.partial(_flash_kernel, tq=tq, nh=nh, scale=scale),
        grid=(bsz, heads // nh, nq),
        in_specs=[qo, kv, kv],
        out_specs=qo,
        out_shape=jax.ShapeDtypeStruct((t, heads * LANES), BF16),
        compiler_params=_cparams(("parallel", "parallel", "arbitrary")),
        name="mla_attention",
    )(q, k, v)


def _route(sb, s):
    per = N_EXPERTS // N_GROUPS
    gscore = []
    for g in range(N_GROUPS):
        a, b, c, d = sb[per * g:per * g + per]
        hi1, lo1 = jnp.maximum(a, b), jnp.minimum(a, b)
        hi2, lo2 = jnp.maximum(c, d), jnp.minimum(c, d)
        top1 = jnp.maximum(hi1, hi2)
        top2 = jnp.maximum(jnp.minimum(hi1, hi2), jnp.maximum(lo1, lo2))
        gscore.append(top1 + top2)
    best, gidx = gscore[0], jnp.zeros(gscore[0].shape, I32)
    for g in range(1, N_GROUPS):
        upd = gscore[g] > best
        best = jnp.where(upd, gscore[g], best)
        gidx = jnp.where(upd, g, gidx)

    def pick(rows, j):
        out = rows[j]
        for g in range(1, N_GROUPS):
            out = jnp.where(gidx == g, rows[per * g + j], out)
        return out

    vals = [pick(sb, j) for j in range(per)]
    raws = [pick(s, j) for j in range(per)]
    b1, i1, s1 = vals[0], jnp.zeros(gidx.shape, I32), raws[0]
    for j in range(1, per):
        upd = vals[j] > b1
        b1 = jnp.where(upd, vals[j], b1)
        i1 = jnp.where(upd, j, i1)
        s1 = jnp.where(upd, raws[j], s1)
    b2, i2, s2 = jnp.zeros_like(b1), jnp.full(gidx.shape, -1, I32), jnp.zeros_like(b1)
    for j in range(per):
        upd = (i1 != j) & ((i2 < 0) | (vals[j] > b2))
        b2 = jnp.where(upd, vals[j], b2)
        i2 = jnp.where(upd, j, i2)
        s2 = jnp.where(upd, raws[j], s2)
    denom = s1 + s2
    return per * gidx + i1, per * gidx + i2, s1 / denom, s2 / denom


def _post_kernel(x_ref, mix_ref, mq_ref, kvm_ref, wo_ref, g_ref, b_ref, rw_ref, rb_ref,
                 x1_ref, esel_ref, gate_ref):
    mq = mq_ref[...]
    kvm = kvm_ref[0]
    mem = []
    for h in range(MEM_HEADS):
        hs = slice(h * MEM_HEAD_DIM, (h + 1) * MEM_HEAD_DIM)
        vs = slice(MEM_WIDTH + h * MEM_HEAD_DIM, MEM_WIDTH + (h + 1) * MEM_HEAD_DIM)
        s = lax.dot_general(mq[:, hs], kvm[:, hs], _NT, preferred_element_type=F32) * (MEM_HEAD_DIM ** -0.5)
        p = jnp.exp(s - jnp.max(s, axis=-1, keepdims=True))
        p = p / jnp.sum(p, axis=-1, keepdims=True)
        mem.append(jnp.dot(p.astype(BF16), kvm[:, vs], preferred_element_type=F32))
    y = jnp.dot(mix_ref[...], wo_ref[0:D_MODEL, :], preferred_element_type=F32)
    for h in range(MEM_HEADS):
        r0 = D_MODEL + h * MEM_HEAD_DIM
        y = y + jnp.dot(mem[h].astype(BF16), wo_ref[r0:r0 + MEM_HEAD_DIM, :], preferred_element_type=F32)
    x1 = _layer_norm(DEEPNORM_ALPHA * x_ref[...] + y, g_ref[...], b_ref[...])
    x1_ref[...] = x1
    logits = lax.dot_general(rw_ref[...], x1.astype(BF16), _NT, preferred_element_type=F32)
    s_all = _sigmoid(logits)
    sb_all = s_all + rb_ref[...]
    e1, e2, w1, w2 = _route([sb_all[e:e + 1, :] for e in range(N_EXPERTS)],
                            [s_all[e:e + 1, :] for e in range(N_EXPERTS)])
    esel_ref[0:1, :] = e1
    esel_ref[1:2, :] = e2
    gate_ref[0:1, :] = w1
    gate_ref[1:2, :] = w2


def _post_mixer(x, mix, mq, kvm, w_o, ln_g, ln_b, rw_t, rbias, seq, tm=256):
    t = x.shape[0]
    tm = min(tm, seq)
    per_seq = seq // tm
    row = lambda width: pl.BlockSpec((tm, width), lambda i: (i, 0))
    full = lambda a: pl.BlockSpec(a.shape, lambda i: (0,) * a.ndim)
    g = ln_g.reshape(1, -1)
    b = ln_b.reshape(1, -1)
    return pl.pallas_call(
        _post_kernel,
        grid=(t // tm,),
        in_specs=[row(D_MODEL), row(D_MODEL), row(MEM_WIDTH),
                  pl.BlockSpec((1,) + kvm.shape[1:], lambda i: (i // per_seq, 0, 0)),
                  full(w_o), full(g), full(b), full(rw_t), full(rbias)],
        out_specs=[row(D_MODEL), pl.BlockSpec((TOPK_EXPERTS, tm), lambda i: (0, i)),
                   pl.BlockSpec((TOPK_EXPERTS, tm), lambda i: (0, i))],
        out_shape=[jax.ShapeDtypeStruct((t, D_MODEL), F32), jax.ShapeDtypeStruct((TOPK_EXPERTS, t), I32),
                   jax.ShapeDtypeStruct((TOPK_EXPERTS, t), F32)],
        compiler_params=_cparams(("parallel",)),
        name="post_mixer",
    )(x, mix, mq, kvm, w_o, g, b, rw_t, rbias)


def _moe_rank_kernel(e_ref, rank_ref, cnt_ref, base_ref, *, tr):
    i = pl.program_id(0)

    @pl.when(i == 0)
    def _():
        base_ref[...] = jnp.zeros(base_ref.shape, F32)

    e1 = e_ref[0:1, :]
    e2 = e_ref[1:2, :]
    row = lax.broadcasted_iota(I32, (N_EXPERTS, tr), 0)
    onehot = jnp.where((row == e1) | (row == e2), 1.0, 0.0)
    before = (lax.broadcasted_iota(I32, (tr, tr), 0) < lax.broadcasted_iota(I32, (tr, tr), 1)).astype(BF16)
    seen = jnp.dot(onehot.astype(BF16), before, preferred_element_type=F32) + base_ref[:, 0:1]
    rank_ref[0:1, :] = jnp.sum(jnp.where(row == e1, seen, 0.0), axis=0, keepdims=True).astype(I32)
    rank_ref[1:2, :] = jnp.sum(jnp.where(row == e2, seen, 0.0), axis=0, keepdims=True).astype(I32)
    base_ref[...] = base_ref[...] + jnp.sum(onehot, axis=1, keepdims=True)
    cnt_ref[...] = base_ref[...]


def _moe_rank(esel, tr=512):
    t = esel.shape[1]
    tr = min(tr, t)
    return pl.pallas_call(
        functools.partial(_moe_rank_kernel, tr=tr),
        grid=(t // tr,),
        in_specs=[pl.BlockSpec((TOPK_EXPERTS, tr), lambda i: (0, i))],
        out_specs=[pl.BlockSpec((TOPK_EXPERTS, tr), lambda i: (0, i)),
                   pl.BlockSpec((N_EXPERTS, LANES), lambda i: (0, 0))],
        out_shape=[jax.ShapeDtypeStruct((TOPK_EXPERTS, t), I32), jax.ShapeDtypeStruct((N_EXPERTS, LANES), F32)],
        scratch_shapes=[pltpu.VMEM((N_EXPERTS, LANES), F32)],
        compiler_params=_cparams(("arbitrary",)),
        name="moe_rank",
    )(esel)


def _moe_scatter_kernel(zero_ref, slot_ref, x_ref, buf_ref, zeros_ref, sem, *, ts, n_chunks):
    i = pl.program_id(0)

    def row_copy(tok, k):
        return pltpu.make_async_copy(x_ref.at[pl.ds(tok, 1)], buf_ref.at[pl.ds(slot_ref[0, k, tok], 1)], sem)

    @pl.when(i == 0)
    def _():
        zeros_ref[...] = jnp.zeros(zeros_ref.shape, F32)

        def fill(c, _):
            @pl.when(zero_ref[c] != 0)
            def _():
                cp = pltpu.make_async_copy(
                    zeros_ref, buf_ref.at[pl.ds(pl.multiple_of(c * MOE_CHUNK, MOE_CHUNK), MOE_CHUNK)], sem)
                cp.start()
                cp.wait()
            return 0

        lax.fori_loop(0, n_chunks, fill, 0)

    def start(tok, _):
        for k in range(TOPK_EXPERTS):
            row_copy(tok, k).start()
        return 0

    def wait(tok, _):
        for k in range(TOPK_EXPERTS):
            row_copy(tok, k).wait()
        return 0

    lax.fori_loop(0, ts, start, 0, unroll=8)
    lax.fori_loop(0, ts, wait, 0, unroll=8)


def _moe_scatter(x1, slots, zero_flags, n_chunks, ts=512):
    t = x1.shape[0]
    grid_spec = pltpu.PrefetchScalarGridSpec(
        num_scalar_prefetch=1,
        grid=(t // ts,),
        in_specs=[pl.BlockSpec((1, TOPK_EXPERTS, ts), lambda i, z: (i, 0, 0), memory_space=pltpu.SMEM),
                  pl.BlockSpec((ts, D_MODEL), lambda i, z: (i, 0))],
        out_specs=pl.BlockSpec(memory_space=pl.ANY),
        scratch_shapes=[pltpu.VMEM((MOE_CHUNK, D_MODEL), F32), pltpu.SemaphoreType.DMA(())],
    )
    return pl.pallas_call(
        functools.partial(_moe_scatter_kernel, ts=ts, n_chunks=n_chunks),
        grid_spec=grid_spec,
        out_shape=jax.ShapeDtypeStruct((n_chunks * MOE_CHUNK, D_MODEL), F32),
        compiler_params=_cparams(("arbitrary",)),
        name="moe_scatter",
    )(zero_flags, slots, x1)


def _moe_ffn_kernel(cexp_ref, nused_ref, x_ref, win_ref, wout_ref, y_ref):
    c = pl.program_id(0)

    @pl.when(c < nused_ref[0])
    def _():
        h = jnp.dot(x_ref[...].astype(BF16), win_ref[0], preferred_element_type=F32)
        gate_in = h[:, :EXPERT_FF]
        act = gate_in * _sigmoid(gate_in) * h[:, EXPERT_FF:]
        y_ref[...] = jnp.dot(act.astype(BF16), wout_ref[0], preferred_element_type=F32)

    @pl.when(c >= nused_ref[0])
    def _():
        y_ref[...] = jnp.zeros(y_ref.shape, F32)


def _moe_ffn(buf, chunk_exp, n_used, w_in, w_out):
    n_chunks = chunk_exp.shape[0]
    grid_spec = pltpu.PrefetchScalarGridSpec(
        num_scalar_prefetch=2,
        grid=(n_chunks,),
        in_specs=[pl.BlockSpec((MOE_CHUNK, D_MODEL), lambda c, ce, nu: (c, 0)),
                  pl.BlockSpec((1, D_MODEL, 2 * EXPERT_FF), lambda c, ce, nu: (ce[c], 0, 0)),
                  pl.BlockSpec((1, EXPERT_FF, D_MODEL), lambda c, ce, nu: (ce[c], 0, 0))],
        out_specs=pl.BlockSpec((MOE_CHUNK, D_MODEL), lambda c, ce, nu: (c, 0)),
    )
    return pl.pallas_call(
        _moe_ffn_kernel,
        grid_spec=grid_spec,
        out_shape=jax.ShapeDtypeStruct(buf.shape, F32),
        compiler_params=_cparams(("arbitrary",)),
        name="moe_ffn",
    )(chunk_exp, n_used, buf, w_in, w_out)


def _moe_combine_kernel(slot_ref, x_ref, gate_ref, g_ref, b_ref, yb_ref, o_ref, rows_ref, sem, *, tc):
    def row_copy(tok, k):
        return pltpu.make_async_copy(yb_ref.at[pl.ds(slot_ref[0, k, tok], 1)], rows_ref.at[k, pl.ds(tok, 1)], sem)

    def start(tok, _):
        for k in range(TOPK_EXPERTS):
            row_copy(tok, k).start()
        return 0

    def wait(tok, _):
        for k in range(TOPK_EXPERTS):
            row_copy(tok, k).wait()
        return 0

    lax.fori_loop(0, tc, start, 0, unroll=8)
    lax.fori_loop(0, tc, wait, 0, unroll=8)
    gates = gate_ref[...]
    y = rows_ref[0] * gates[:, 0:1] + rows_ref[1] * gates[:, 1:2]
    o_ref[...] = _layer_norm(DEEPNORM_ALPHA * x_ref[...] + y, g_ref[...], b_ref[...])


def _moe_combine(x1, gates, slots, yb, ln_g, ln_b, tc=512):
    t = x1.shape[0]
    g = ln_g.reshape(1, -1)
    b = ln_b.reshape(1, -1)
    vec = pl.BlockSpec((1, D_MODEL), lambda i: (0, 0))
    return pl.pallas_call(
        functools.partial(_moe_combine_kernel, tc=tc),
        grid=(t // tc,),
        in_specs=[pl.BlockSpec((1, TOPK_EXPERTS, tc), lambda i: (i, 0, 0), memory_space=pltpu.SMEM),
                  pl.BlockSpec((tc, D_MODEL), lambda i: (i, 0)),
                  pl.BlockSpec((tc, LANES), lambda i: (i, 0)),
                  vec, vec,
                  pl.BlockSpec(memory_space=pl.ANY)],
        out_specs=pl.BlockSpec((tc, D_MODEL), lambda i: (i, 0)),
        out_shape=jax.ShapeDtypeStruct((t, D_MODEL), F32),
        scratch_shapes=[pltpu.VMEM((TOPK_EXPERTS, tc, D_MODEL), F32), pltpu.SemaphoreType.DMA(())],
        compiler_params=_cparams(("arbitrary",)),
        name="moe_combine",
    )(slots, x1, gates, g, b, yb)


def _moe(x1, esel, gate_t, w_in, w_out, ln_g, ln_b, tile=512):
    t = x1.shape[0]
    tile = min(tile, t)
    n_chunks = -(-t * TOPK_EXPERTS // MOE_CHUNK) + N_EXPERTS
    rank, cnt = _moe_rank(esel)
    counts = cnt[:, 0].astype(I32)
    padded = (counts + MOE_CHUNK - 1) // MOE_CHUNK * MOE_CHUNK
    pad_end = jnp.cumsum(padded)
    pad_start = pad_end - padded
    expert = jnp.arange(N_EXPERTS, dtype=I32)
    slot = rank + jnp.sum(jnp.where(esel[None] == expert[:, None, None], pad_start[:, None, None], 0), axis=0)
    slots = slot.reshape(TOPK_EXPERTS, t // tile, tile).transpose(1, 0, 2)
    chunk_lo = jnp.arange(n_chunks, dtype=I32) * MOE_CHUNK
    chunk_exp = jnp.minimum(jnp.sum((pad_end[None, :] <= chunk_lo[:, None]).astype(I32), axis=1), N_EXPERTS - 1)
    n_used = (pad_end[-1] // MOE_CHUNK).astype(I32).reshape(1)
    is_last = jnp.any((chunk_lo[:, None] + MOE_CHUNK) == pad_end[None, :], axis=1)
    zero_flags = (is_last | (chunk_lo >= pad_end[-1])).astype(I32)
    buf = _moe_scatter(x1, slots, zero_flags, n_chunks, ts=tile)
    yb = _moe_ffn(buf, chunk_exp, n_used, w_in, w_out)
    gates = _pad_cols(gate_t.T, LANES)
    return _moe_combine(x1, gates, slots, yb, ln_g, ln_b, tc=tile)


def _pad_cols(w, width, offset=0):
    out = jnp.zeros((w.shape[0], width), w.dtype)
    return out.at[:, offset:offset + w.shape[1]].set(w)


def _rope_rows(half, lane_lo, lane_hi):
    inv_freq = ROPE_THETA ** (-jnp.arange(half, dtype=F32) / half)
    lane = jnp.arange(LANES)
    active = (lane >= lane_lo) & (lane < lane_hi)
    pos_in = (lane - lane_lo) % (2 * half)
    freq = jnp.where(active, inv_freq[pos_in % half], 0.0).astype(F32)
    sign = jnp.where(active, jnp.where(pos_in < half, -1.0, 1.0), 0.0).astype(F32)
    return freq.reshape(1, LANES), sign.reshape(1, LANES)


def kernel(x, mem, positions, router_w, router_bias, l0_w_in, l0_mem_wkv, l0_w_o, l0_ln_g, l0_ln_b, l0_moe_w_in, l0_moe_w_out, l1_w_in, l1_conv_w, l1_conv_b, l1_conv_norm_g, l1_conv_norm_b, l1_mem_wkv, l1_w_o, l1_ln_g, l1_ln_b, l1_moe_w_in, l1_moe_w_out, l2_w_in, l2_mem_wkv, l2_w_o, l2_ln_g, l2_ln_b, l2_moe_w_in, l2_moe_w_out, l3_w_in, l3_q_norm_g, l3_kv_norm_g, l3_w_q_up, l3_w_kv_up, l3_mem_wkv, l3_w_o, l3_ln_g, l3_ln_b, l3_moe_w_in, l3_moe_w_out):
    bsz, seq, d = x.shape
    t = bsz * seq
    n_mem = mem.shape[1]
    xt = x.reshape(t, d)
    mem2 = mem.reshape(bsz * n_mem, d)
    pos_b = jnp.broadcast_to(positions.astype(F32).reshape(t, 1), (t, LANES))
    rw_t = router_w.T.astype(BF16)
    rbias = router_bias.astype(F32).reshape(N_EXPERTS, 1)

    def finish_layer(xt, mix, mq, mem_wkv, w_o, ln_g, ln_b, moe_w_in, moe_w_out):
        (kvm,) = _proj(mem2, mem_wkv.astype(BF16), [(0, 2 * MEM_WIDTH, 2 * MEM_WIDTH, 0)], [BF16],
                       tm=n_mem, name="mem_kv")
        kvm = kvm.reshape(bsz, n_mem, 2 * MEM_WIDTH)
        x1, esel, gate_t = _post_mixer(xt, mix, mq, kvm, w_o.astype(BF16), ln_g[0], ln_b[0], rw_t, rbias, seq)
        return _moe(x1, esel, gate_t, moe_w_in.astype(BF16), moe_w_out.astype(BF16), ln_g[1], ln_b[1])

    w0 = l0_w_in.astype(BF16)
    n0 = w0.shape[1]
    (h0,) = _proj(xt, w0, [(0, n0, n0, 0)], [BF16], name="proj0")
    mix = _sb_attention(h0, bsz, seq)
    mq = h0[:, 3 * D_MODEL:]
    xt = finish_layer(xt, mix, mq, l0_mem_wkv, l0_w_o, l0_ln_g, l0_ln_b, l0_moe_w_in, l0_moe_w_out)

    w1 = l1_w_in.astype(BF16)
    a, g, mq = _proj(xt, w1, [(0, D_MODEL, D_MODEL, 0), (D_MODEL, D_MODEL, D_MODEL, 0),
                              (2 * D_MODEL, MEM_WIDTH, MEM_WIDTH, 0)], [F32, F32, BF16], name="proj1")
    mix = _conformer_conv(a, g, l1_conv_w, l1_conv_b, l1_conv_norm_g, l1_conv_norm_b, seq)
    xt = finish_layer(xt, mix, mq, l1_mem_wkv, l1_w_o, l1_ln_g, l1_ln_b, l1_moe_w_in, l1_moe_w_out)

    w2 = l2_w_in
    c = 0
    parts = []
    for width in (DSA_HEADS * DSA_HEAD_DIM, DSA_HEAD_DIM, DSA_HEAD_DIM, IDX_HEADS * IDX_DIM, IDX_DIM, IDX_HEADS,
                  MEM_WIDTH):
        parts.append(w2[:, c:c + width])
        c += width
    wq, wk, wv, wqi, wki, wwi, wmq = parts
    w2p = jnp.concatenate([wq, _pad_cols(wk, LANES), _pad_cols(wv, LANES), wqi, _pad_cols(wki, LANES),
                           _pad_cols(wwi, LANES), wmq], axis=1).astype(BF16)
    freq, sign = _rope_rows(DSA_HEAD_DIM // 2, 0, LANES)
    cos_a, sin_a = _rope_tables(pos_b, freq, sign)
    rh = DSA_HEAD_DIM // 2
    groups = [(0, 1024, 1024, rh), (1024, LANES, DSA_HEAD_DIM, rh), (1152, LANES, LANES, 0),
              (1280, 512, 512, rh), (1792, LANES, IDX_DIM, rh), (1920, LANES, LANES, 0),
              (2048, MEM_WIDTH, MEM_WIDTH, 0)]
    q, k, v, qi, ki, wi, mq = _proj(xt, w2p, groups, [BF16, BF16, BF16, BF16, BF16, F32, BF16],
                                    rope=(cos_a, sin_a), name="proj2")
    mix = _dsa_attention(q, qi, wi, k, v, ki, bsz, seq)
    xt = finish_layer(xt, mix, mq, l2_mem_wkv, l2_w_o, l2_ln_g, l2_ln_b, l2_moe_w_in, l2_moe_w_out)

    w3 = l3_w_in
    rope_lo = MLA_NOPE
    w_cq = w3[:, :MLA_Q_RANK]
    w_ckv = w3[:, MLA_Q_RANK:MLA_Q_RANK + MLA_KV_RANK]
    w_kr = w3[:, MLA_Q_RANK + MLA_KV_RANK:MLA_Q_RANK + MLA_KV_RANK + MLA_ROPE]
    w_mq = w3[:, MLA_Q_RANK + MLA_KV_RANK + MLA_ROPE:]
    w3p = jnp.concatenate([w_cq, w_ckv, _pad_cols(w_kr, LANES, rope_lo), w_mq], axis=1).astype(BF16)
    freq, sign = _rope_rows(MLA_ROPE // 2, rope_lo, rope_lo + MLA_ROPE)
    cos_b, sin_b = _rope_tables(pos_b, freq, sign)
    groups = [(0, MLA_Q_RANK, MLA_Q_RANK, 0), (MLA_Q_RANK, MLA_KV_RANK, MLA_KV_RANK, 0),
              (MLA_Q_RANK + MLA_KV_RANK, LANES, LANES, MLA_ROPE // 2),
              (MLA_Q_RANK + MLA_KV_RANK + LANES, MEM_WIDTH, MEM_WIDTH, 0)]
    c_q, c_kv, kr, mq = _proj(xt, w3p, groups, [F32, F32, F32, BF16], rope=(cos_b, sin_b), name="proj3")
    qk_dim = MLA_NOPE + MLA_ROPE
    wq_heads = l3_w_q_up.reshape(MLA_Q_RANK, MLA_HEADS, qk_dim)
    wq_p = jnp.zeros((MLA_Q_RANK, MLA_HEADS, LANES), F32).at[:, :, :qk_dim].set(wq_heads)
    wkv_heads = l3_w_kv_up.reshape(MLA_KV_RANK, MLA_HEADS, MLA_NOPE + MLA_V)
    wk_p = jnp.zeros((MLA_KV_RANK, MLA_HEADS, LANES), F32).at[:, :, :MLA_NOPE].set(wkv_heads[:, :, :MLA_NOPE])
    wv_p = wkv_heads[:, :, MLA_NOPE:]
    flat = lambda w: w.reshape(w.shape[0], MLA_HEADS * LANES).astype(BF16)
    qh, kh, vh = _mla_prep(c_q, c_kv, kr, cos_b, sin_b, l3_q_norm_g, l3_kv_norm_g, flat(wq_p), flat(wk_p),
                           flat(wv_p))
    mix = _causal_attention(qh, kh, vh, bsz, seq, MLA_HEADS, qk_dim ** -0.5)
    xt = finish_layer(xt, mix, mq, l3_mem_wkv, l3_w_o, l3_ln_g, l3_ln_b, l3_moe_w_in, l3_moe_w_out)
    return xt.reshape(bsz, seq, d)
```

```python
import functools

import jax
import jax.numpy as jnp
from jax import lax
from jax.experimental import pallas as pl
from jax.experimental.pallas import tpu as pltpu

F32 = jnp.float32
BF16 = jnp.bfloat16
I32 = jnp.int32

D_MODEL = 1024
DEPTH = 4
ROPE_THETA = 10000.0
LN_EPS = 1e-5
RMS_EPS = 1e-6
DEEPNORM_ALPHA = (2.0 * DEPTH) ** 0.25
SB_HEADS = 8
CONV_WIDTH = 31
DSA_HEADS = 16
DSA_HEAD_DIM = 64
IDX_HEADS = 8
IDX_DIM = 64
IDX_TOPK_MAX = 256
MLA_HEADS = 8
MLA_Q_RANK = 384
MLA_KV_RANK = 256
MLA_NOPE = 64
MLA_ROPE = 32
MLA_V = 128
MEM_HEADS = 4
MEM_HEAD_DIM = 64
MEM_WIDTH = 256
N_EXPERTS = 16
N_GROUPS = 4
EXPERT_FF = 512
TOPK_EXPERTS = 2
MOE_CHUNK = 512

LANES = 128
VMEM_LIMIT_BYTES = 56 * 1024 * 1024
NEG_BIG = -1e30
MAX_SAFE_SHIFT = 40.0
INT_MIN = -2 ** 31

_NT = (((1,), (1,)), ((), ()))


def _cparams(sem):
    return pltpu.CompilerParams(dimension_semantics=sem, vmem_limit_bytes=VMEM_LIMIT_BYTES)


def _layer_norm(v, g, b):
    mu = jnp.mean(v, axis=-1, keepdims=True)
    d = v - mu
    var = jnp.mean(d * d, axis=-1, keepdims=True)
    return d * lax.rsqrt(var + LN_EPS) * g + b


def _sigmoid(v):
    return 1.0 / (1.0 + jnp.exp(-v))


def _rope_table_kernel(pos_ref, freq_ref, sign_ref, cos_ref, sin_ref):
    ang = pos_ref[...] * freq_ref[...]
    cos_ref[...] = jnp.cos(ang)
    sin_ref[...] = jnp.sin(ang) * sign_ref[...]


def _rope_tables(pos_b, freq_row, sign_row, tm=512):
    t = pos_b.shape[0]
    row = pl.BlockSpec((tm, LANES), lambda i: (i, 0))
    one = pl.BlockSpec((1, LANES), lambda i: (0, 0))
    return pl.pallas_call(
        _rope_table_kernel,
        grid=(t // tm,),
        in_specs=[row, one, one],
        out_specs=[row, row],
        out_shape=[jax.ShapeDtypeStruct((t, LANES), F32)] * 2,
        compiler_params=_cparams(("parallel",)),
        name="rope_tables",
    )(pos_b, freq_row, sign_row)


def _rope128(v, cos, sin, half):
    lane = lax.broadcasted_iota(I32, v.shape, 1)
    first = (lane & (2 * half - 1)) < half
    fwd = pltpu.roll(v, LANES - half, 1)
    bwd = pltpu.roll(v, half, 1)
    return v * cos + jnp.where(first, fwd, bwd) * sin


def _proj_kernel(*refs, groups, has_rope):
    x_ref, w_ref = refs[0], refs[1]
    nin = 4 if has_rope else 2
    outs = refs[nin:]
    x = x_ref[...].astype(BF16)
    for (start, width, out_width, rope_half), o_ref in zip(groups, outs):
        acc = jnp.dot(x, w_ref[:, start:start + width], preferred_element_type=F32)
        if rope_half:
            cos = refs[2][...]
            sin = refs[3][...]
            for c in range(width // LANES):
                r = _rope128(acc[:, c * LANES:(c + 1) * LANES], cos, sin, rope_half)
                lo = c * LANES
                hi = min(lo + LANES, out_width)
                o_ref[:, lo:hi] = r[:, :hi - lo].astype(o_ref.dtype)
        else:
            o_ref[...] = acc[:, :out_width].astype(o_ref.dtype)


def _proj(x, w, groups, out_dtypes, rope=None, tm=512, name="proj"):
    m, k = x.shape
    n = w.shape[1]
    tm = min(tm, m)
    in_specs = [pl.BlockSpec((tm, k), lambda i: (i, 0)), pl.BlockSpec((k, n), lambda i: (0, 0))]
    args = [x, w]
    if rope is not None:
        in_specs += [pl.BlockSpec((tm, LANES), lambda i: (i, 0))] * 2
        args += list(rope)
    out_specs = [pl.BlockSpec((tm, g[2]), lambda i: (i, 0)) for g in groups]
    out_shape = [jax.ShapeDtypeStruct((m, g[2]), dt) for g, dt in zip(groups, out_dtypes)]
    return pl.pallas_call(
        functools.partial(_proj_kernel, groups=tuple(groups), has_rope=rope is not None),
        grid=(m // tm,),
        in_specs=in_specs,
        out_specs=out_specs,
        out_shape=out_shape,
        compiler_params=_cparams(("parallel",)),
        name=name,
    )(*args)


def _sb_kernel(q_ref, k_ref, v_ref, o_ref, *, tq, ck, nh, scale):
    i = pl.program_id(2)
    r = lax.broadcasted_iota(I32, (ck, ck), 0)
    c = lax.broadcasted_iota(I32, (ck, ck), 1)
    suffix = (r >= c).astype(BF16)
    row = lax.broadcasted_iota(I32, (tq, ck), 0)
    col = lax.broadcasted_iota(I32, (tq, ck), 1)

    def head_block(hh, start, carry, diag):
        tail, acc = carry
        hs = slice(hh * LANES, (hh + 1) * LANES)
        kb = k_ref[pl.ds(start, tq), hs]
        z = lax.dot_general(q_ref[:, hs], kb, _NT, preferred_element_type=F32) * scale
        for cc in reversed(range(tq // ck)):
            zc = z[:, cc * ck:(cc + 1) * ck]
            sp = jnp.maximum(zc, 0.0) + jnp.log(1.0 + jnp.exp(-jnp.abs(zc)))
            if diag:
                strict = col + cc * ck < row
                sp = jnp.where(strict, sp, 0.0)
            cum = jnp.dot(sp.astype(BF16), suffix, preferred_element_type=F32)
            a = jnp.exp(zc - cum - tail)
            if diag:
                a = jnp.where(strict, a, 0.0)
            vb = v_ref[pl.ds(pl.multiple_of(start + cc * ck, ck), ck), hs]
            acc = acc + jnp.dot(a.astype(BF16), vb, preferred_element_type=F32)
            tail = tail + jnp.sum(sp, axis=-1, keepdims=True)
        return tail, acc

    def block(j, carries, diag):
        start = pl.multiple_of(j * tq, tq)
        return tuple(head_block(hh, start, carries[hh], diag) for hh in range(nh))

    carries = tuple((jnp.zeros((tq, 1), F32), jnp.zeros((tq, LANES), F32)) for _ in range(nh))
    carries = block(i, carries, True)
    carries = lax.fori_loop(0, i, lambda jj, c: block(i - 1 - jj, c, False), carries)
    for hh in range(nh):
        o_ref[:, hh * LANES:(hh + 1) * LANES] = carries[hh][1].astype(o_ref.dtype)


def _sb_attention(h, bsz, seq, tq=512, ck=256, nh=2):
    tq = min(tq, seq)
    ck = min(ck, tq)
    nq = seq // tq
    t = bsz * seq
    hg = SB_HEADS // nh
    wide = nh * LANES
    return pl.pallas_call(
        functools.partial(_sb_kernel, tq=tq, ck=ck, nh=nh, scale=LANES ** -0.5),
        grid=(bsz, hg, nq),
        in_specs=[
            pl.BlockSpec((tq, wide), lambda b, hh, i: (b * nq + i, hh)),
            pl.BlockSpec((seq, wide), lambda b, hh, i: (b, hg + hh), pipeline_mode=pl.Buffered(1)),
            pl.BlockSpec((seq, wide), lambda b, hh, i: (b, 2 * hg + hh), pipeline_mode=pl.Buffered(1)),
        ],
        out_specs=pl.BlockSpec((tq, wide), lambda b, hh, i: (b * nq + i, hh)),
        out_shape=jax.ShapeDtypeStruct((t, SB_HEADS * LANES), BF16),
        compiler_params=_cparams(("parallel", "parallel", "arbitrary")),
        name="sb_attention",
    )(h, h, h)


def _conv_kernel(a_ref, g_ref, ap_ref, gp_ref, w_ref, cb_ref, ng_ref, nb_ref, o_ref, hext_ref, y_ref,
                 *, tm, halo, tiles_per_seq):
    i = pl.program_id(0)
    hext_ref[halo:halo + tm, :] = a_ref[...] * _sigmoid(g_ref[...])
    prev = ap_ref[...] * _sigmoid(gp_ref[...])
    first = (i % tiles_per_seq) == 0
    hext_ref[0:halo, :] = jnp.where(first, 0.0, prev)
    off = halo - (CONV_WIDTH - 1)
    for c in range(D_MODEL // LANES):
        cs = slice(c * LANES, (c + 1) * LANES)
        acc = jnp.zeros((tm, LANES), F32)
        for j in range(CONV_WIDTH):
            acc = acc + w_ref[j:j + 1, cs] * hext_ref[off + j:off + j + tm, cs]
        y_ref[:, cs] = acc
    y = y_ref[...] + cb_ref[...]
    y = _layer_norm(y, ng_ref[...], nb_ref[...])
    o_ref[...] = (y * _sigmoid(y)).astype(o_ref.dtype)


def _conformer_conv(a, g, conv_w, conv_b, norm_g, norm_b, seq, tm=256, halo=32):
    t = a.shape[0]
    tm = min(tm, seq)
    r = tm // halo
    cur = pl.BlockSpec((tm, D_MODEL), lambda i: (i, 0))
    prv = pl.BlockSpec((halo, D_MODEL), lambda i: (jnp.maximum(i * r - 1, 0), 0))
    vec = pl.BlockSpec((1, D_MODEL), lambda i: (0, 0))
    return pl.pallas_call(
        functools.partial(_conv_kernel, tm=tm, halo=halo, tiles_per_seq=seq // tm),
        grid=(t // tm,),
        in_specs=[cur, cur, prv, prv, pl.BlockSpec((CONV_WIDTH, D_MODEL), lambda i: (0, 0)), vec, vec, vec],
        out_specs=cur,
        out_shape=jax.ShapeDtypeStruct((t, D_MODEL), BF16),
        scratch_shapes=[pltpu.VMEM((tm + halo, D_MODEL), F32), pltpu.VMEM((tm, D_MODEL), F32)],
        compiler_params=_cparams(("parallel",)),
        name="conformer_conv",
    )(a, g, a, g, conv_w, conv_b.reshape(1, -1), norm_g.reshape(1, -1), norm_b.reshape(1, -1))


def _dsa_kernel(q_ref, qi_ref, wi_ref, k_ref, v_ref, ki_ref, o_ref,
                qs_ref, qis_ref, keys_ref, eq_hi_ref, mrun_ref, mb_ref, acc_ref, kmax2_ref,
                *, tq, tkb, topk, seq, idx_bits, scale):
    i = pl.program_id(1)
    nkb = (i * tq + tq + tkb - 1) // tkb

    @pl.when(i == 0)
    def _():
        rows = min(tkb, seq)

        def body(c, m):
            kk = k_ref[pl.ds(pl.multiple_of(c * rows, rows), rows), :].astype(F32)
            return jnp.maximum(m, jnp.sum(kk * kk, axis=-1, keepdims=True))

        kmax2_ref[0] = jnp.max(lax.fori_loop(0, seq // rows, body, jnp.zeros((rows, 1), F32)))

    for h in range(DSA_HEADS):
        qs_ref[h * tq:(h + 1) * tq, :] = q_ref[:, h * DSA_HEAD_DIM:(h + 1) * DSA_HEAD_DIM]
    for h in range(IDX_HEADS):
        qis_ref[h * tq:(h + 1) * tq, :] = qi_ref[:, h * IDX_DIM:(h + 1) * IDX_DIM]
    wi = wi_ref[...] * (IDX_HEADS ** -0.5 * IDX_DIM ** -0.5)
    wcols = [wi[:, h:h + 1] for h in range(IDX_HEADS)]
    q_pos = i * tq + lax.broadcasted_iota(I32, (tq, 1), 0)
    col0 = lax.broadcasted_iota(I32, (tq, tkb), 1)

    def scores(jb, _):
        start = pl.multiple_of(jb * tkb, tkb)
        kib = ki_ref[pl.ds(start, tkb), :]
        dots = lax.dot_general(qis_ref[...], kib, _NT, preferred_element_type=F32)
        d3 = dots.reshape(IDX_HEADS, tq, tkb)
        sc = wcols[0] * jnp.maximum(d3[0], 0.0)
        for h in range(1, IDX_HEADS):
            sc = sc + wcols[h] * jnp.maximum(d3[h], 0.0)
        sc = sc + 0.0
        bits = lax.bitcast_convert_type(sc, I32)
        key = bits ^ ((bits >> 31) & 0x7FFFFFFF)
        key = jnp.where(col0 + start <= q_pos, key, INT_MIN)
        keys_ref[:, pl.ds(start, tkb)] = key
        return 0

    lax.fori_loop(0, nkb, scores, 0)

    def count_if(pred):
        def body(c, acc):
            start = pl.multiple_of(c * tkb, tkb)
            ind = jnp.where(pred(keys_ref[:, pl.ds(start, tkb)], start), 1.0, 0.0)
            for s in range(tkb // LANES):
                acc = acc + ind[:, s * LANES:(s + 1) * LANES]
            return acc
        acc = lax.fori_loop(0, nkb, body, jnp.zeros((tq, LANES), F32))
        return jnp.sum(acc, axis=-1, keepdims=True)

    kf = float(topk)
    short = count_if(lambda kb, st: kb > INT_MIN) < kf
    n_pos = count_if(lambda kb, st: kb >= 0)
    tau = jnp.where(n_pos >= kf, 0, INT_MIN).astype(I32)
    n_ge = jnp.where(n_pos >= kf, n_pos, 2.0 * seq)

    def unsettled(n_ge):
        return jnp.max(jnp.where(short | (n_ge == kf), 0.0, 1.0))

    def bisect(state):
        it, tau, n_ge, _ = state
        cand = tau + jnp.left_shift(jnp.int32(1), 30 - it)
        n_cand = count_if(lambda kb, st: kb >= cand)
        keep = n_cand >= kf
        tau = jnp.where(keep, cand, tau)
        n_ge = jnp.where(keep, n_cand, n_ge)
        return it + 1, tau, n_ge, unsettled(n_ge)

    _, tau, n_ge, _ = lax.while_loop(lambda st: (st[0] < 31) & (st[3] > 0.0), bisect,
                                     (jnp.int32(0), tau, n_ge, unsettled(n_ge)))
    n_gt = count_if(lambda kb, st: kb > tau)
    need = kf - n_gt
    eq_hi_ref[...] = jnp.where(short, -1, seq).astype(I32)
    tied = jnp.max(jnp.where((n_ge > kf) & jnp.logical_not(short), 1.0, 0.0)) > 0.0

    @pl.when(tied)
    def _():
        def step(it, p):
            cand = p + jnp.left_shift(jnp.int32(1), idx_bits - it)
            cnt = count_if(lambda kb, st: (kb == tau) & (col0 + st < cand))
            return jnp.where((cand <= seq) & (cnt < need), cand, p)

        p = lax.fori_loop(0, idx_bits + 1, step, jnp.zeros((tq, 1), I32))
        eq_hi_ref[...] = jnp.where(short, -1, p)

    eq_hi = eq_hi_ref[...]

    def selected(start):
        kb = keys_ref[:, pl.ds(start, tkb)]
        return (kb > tau) | ((kb == tau) & (col0 + start <= eq_hi))

    qf = qs_ref[...].astype(F32)
    q_norm2 = jnp.sum(qf * qf, axis=-1, keepdims=True).reshape(DSA_HEADS, tq, 1)
    bound = jnp.sqrt(q_norm2 * kmax2_ref[0]) * scale
    bound_ok = jnp.max(bound) < MAX_SAFE_SHIFT

    @pl.when(bound_ok)
    def _():
        mb_ref[...] = jnp.broadcast_to(bound, mb_ref.shape)

    @pl.when(jnp.logical_not(bound_ok))
    def _():
        mrun_ref[...] = jnp.full(mrun_ref.shape, NEG_BIG, F32)

        def row_max(jb, _):
            start = pl.multiple_of(jb * tkb, tkb)
            s = lax.dot_general(qs_ref[...], k_ref[pl.ds(start, tkb), :], _NT, preferred_element_type=F32)
            s3 = jnp.where(selected(start)[None], s.reshape(DSA_HEADS, tq, tkb), NEG_BIG)
            m = mrun_ref[...]
            for c in range(tkb // LANES):
                m = jnp.maximum(m, s3[:, :, c * LANES:(c + 1) * LANES])
            mrun_ref[...] = m
            return 0

        lax.fori_loop(0, nkb, row_max, 0)
        m_row = jnp.max(mrun_ref[...], axis=-1, keepdims=True) * scale
        mb_ref[...] = jnp.broadcast_to(m_row, mb_ref.shape)

    acc_ref[...] = jnp.zeros(acc_ref.shape, F32)
    vlane = lax.broadcasted_iota(I32, (tkb, LANES), 1)

    def attend(jb, _):
        start = pl.multiple_of(jb * tkb, tkb)
        kb = k_ref[pl.ds(start, tkb), :]
        vb = jnp.where(vlane < DSA_HEAD_DIM, v_ref[pl.ds(start, tkb), :], 1.0).astype(BF16)
        s = lax.dot_general(qs_ref[...], kb, _NT, preferred_element_type=F32)
        t3 = s.reshape(DSA_HEADS, tq, tkb) * scale - mb_ref[...]
        p = jnp.where(selected(start)[None], jnp.exp(t3), 0.0)
        acc_ref[...] += jnp.dot(p.reshape(DSA_HEADS * tq, tkb).astype(BF16), vb, preferred_element_type=F32)
        return 0

    lax.fori_loop(0, nkb, attend, 0)
    acc = acc_ref[...]
    out = acc / pltpu.roll(acc, LANES - DSA_HEAD_DIM, 1)
    for h in range(DSA_HEADS):
        o_ref[:, h * DSA_HEAD_DIM:(h + 1) * DSA_HEAD_DIM] = (
            out[h * tq:(h + 1) * tq, :DSA_HEAD_DIM].astype(o_ref.dtype))


def _dsa_attention(q, qi, wi, k, v, ki, bsz, seq, tq=128, tkb=512):
    t = bsz * seq
    tq = min(tq, seq)
    tkb = min(tkb, seq)
    nq = seq // tq
    topk = min(IDX_TOPK_MAX, seq // 4)
    idx_bits = max(seq - 1, 1).bit_length()
    qrow = lambda width: pl.BlockSpec((tq, width), lambda b, i: (b * nq + i, 0))
    kv = lambda width: pl.BlockSpec((seq, width), lambda b, i: (b, 0), pipeline_mode=pl.Buffered(1))
    return pl.pallas_call(
        functools.partial(_dsa_kernel, tq=tq, tkb=tkb, topk=topk, seq=seq, idx_bits=idx_bits,
                          scale=DSA_HEAD_DIM ** -0.5),
        grid=(bsz, nq),
        in_specs=[qrow(DSA_HEADS * DSA_HEAD_DIM), qrow(IDX_HEADS * IDX_DIM), qrow(LANES),
                  kv(DSA_HEAD_DIM), kv(LANES), kv(IDX_DIM)],
        out_specs=qrow(DSA_HEADS * DSA_HEAD_DIM),
        out_shape=jax.ShapeDtypeStruct((t, DSA_HEADS * DSA_HEAD_DIM), BF16),
        scratch_shapes=[
            pltpu.VMEM((DSA_HEADS * tq, DSA_HEAD_DIM), BF16),
            pltpu.VMEM((IDX_HEADS * tq, IDX_DIM), BF16),
            pltpu.VMEM((tq, seq), I32),
            pltpu.VMEM((tq, 1), I32),
            pltpu.VMEM((DSA_HEADS, tq, LANES), F32),
            pltpu.VMEM((DSA_HEADS, tq, tkb), F32),
            pltpu.VMEM((DSA_HEADS * tq, LANES), F32),
            pltpu.SMEM((1,), F32),
        ],
        compiler_params=_cparams(("parallel", "arbitrary")),
        name="dsa_attention",
    )(q, qi, wi, k, v, ki)


def _mla_prep_kernel(cq_ref, ckv_ref, kr_ref, cos_ref, sin_ref, qg_ref, kvg_ref, wq_ref, wk_ref, wv_ref,
                     q_ref, k_ref, v_ref):
    def rms(c, g):
        return c * lax.rsqrt(jnp.mean(c * c, axis=-1, keepdims=True) + RMS_EPS) * g

    cos = cos_ref[...]
    sin = sin_ref[...]
    qn = rms(cq_ref[...], qg_ref[...]).astype(BF16)
    kvn = rms(ckv_ref[...], kvg_ref[...]).astype(BF16)
    kr = kr_ref[...]
    for h in range(MLA_HEADS):
        hs = slice(h * LANES, (h + 1) * LANES)
        qh = jnp.dot(qn, wq_ref[:, hs], preferred_element_type=F32)
        q_ref[:, hs] = _rope128(qh, cos, sin, MLA_ROPE // 2).astype(q_ref.dtype)
        kh = jnp.dot(kvn, wk_ref[:, hs], preferred_element_type=F32)
        k_ref[:, hs] = (kh + kr).astype(k_ref.dtype)
        v_ref[:, hs] = jnp.dot(kvn, wv_ref[:, hs], preferred_element_type=F32).astype(v_ref.dtype)


def _mla_prep(c_q, c_kv, kr, cos, sin, q_norm_g, kv_norm_g, wq, wk, wv, tm=512):
    t = c_q.shape[0]
    tm = min(tm, t)
    row = lambda width: pl.BlockSpec((tm, width), lambda i: (i, 0))
    full = lambda a: pl.BlockSpec(a.shape, lambda i: (0, 0))
    qg = q_norm_g.reshape(1, -1)
    kvg = kv_norm_g.reshape(1, -1)
    width = MLA_HEADS * LANES
    return pl.pallas_call(
        _mla_prep_kernel,
        grid=(t // tm,),
        in_specs=[row(MLA_Q_RANK), row(MLA_KV_RANK), row(LANES), row(LANES), row(LANES),
                  full(qg), full(kvg), full(wq), full(wk), full(wv)],
        out_specs=[row(width)] * 3,
        out_shape=[jax.ShapeDtypeStruct((t, width), BF16)] * 3,
        compiler_params=_cparams(("parallel",)),
        name="mla_prep",
    )(c_q, c_kv, kr, cos, sin, qg, kvg, wq, wk, wv)


def _flash_kernel(q_ref, k_ref, v_ref, o_ref, *, tq, nh, scale):
    i = pl.program_id(2)
    row = lax.broadcasted_iota(I32, (tq, tq), 0)
    col = lax.broadcasted_iota(I32, (tq, tq), 1)
    causal = col <= row
    ones = jnp.ones((tq, LANES), BF16)

    def head_block(hh, start, carry, diag):
        m, acc = carry
        hs = slice(hh * LANES, (hh + 1) * LANES)
        kb = k_ref[pl.ds(start, tq), hs]
        vb = jnp.concatenate([v_ref[pl.ds(start, tq), hs], ones], axis=1)
        s = lax.dot_general(q_ref[:, hs], kb, _NT, preferred_element_type=F32) * scale
        if diag:
            s = jnp.where(causal, s, NEG_BIG)
        m_new = jnp.maximum(m, jnp.max(s, axis=-1, keepdims=True))
        alpha = jnp.exp(m - m_new)
        p = jnp.exp(s - m_new)
        acc = alpha * acc + jnp.dot(p.astype(BF16), vb, preferred_element_type=F32)
        return m_new, acc

    def block(j, carries, diag):
        start = pl.multiple_of(j * tq, tq)
        return tuple(head_block(hh, start, carries[hh], diag) for hh in range(nh))

    carries = tuple((jnp.full((tq, 1), NEG_BIG, F32), jnp.zeros((tq, 2 * LANES), F32)) for _ in range(nh))
    carries = block(i, carries, True)
    carries = lax.fori_loop(0, i, lambda jj, c: block(i - 1 - jj, c, False), carries)
    for hh in range(nh):
        acc = carries[hh][1]
        o_ref[:, hh * LANES:(hh + 1) * LANES] = (acc[:, :LANES] / acc[:, LANES:]).astype(o_ref.dtype)


def _causal_attention(q, k, v, bsz, seq, heads, scale, tq=512, nh=4):
    tq = min(tq, seq)
    nq = seq // tq
    t = bsz * seq
    wide = nh * LANES
    kv = pl.BlockSpec((seq, wide), lambda b, hh, i: (b, hh), pipeline_mode=pl.Buffered(1))
    qo = pl.BlockSpec((tq, wide), lambda b, hh, i: (b * nq + i, hh))
    return pl.pallas_call(
        functools.partial(_flash_kernel, tq=tq, nh=nh, scale=scale),
        grid=(bsz, heads // nh, nq),
        in_specs=[qo, kv, kv],
        out_specs=qo,
        out_shape=jax.ShapeDtypeStruct((t, heads * LANES), BF16),
        compiler_params=_cparams(("parallel", "parallel", "arbitrary")),
        name="mla_attention",
    )(q, k, v)


def _route(sb, s):
    per = N_EXPERTS // N_GROUPS
    gscore = []
    for g in range(N_GROUPS):
        a, b, c, d = sb[per * g:per * g + per]
        hi1, lo1 = jnp.maximum(a, b), jnp.minimum(a, b)
        hi2, lo2 = jnp.maximum(c, d), jnp.minimum(c, d)
        top1 = jnp.maximum(hi1, hi2)
        top2 = jnp.maximum(jnp.minimum(hi1, hi2), jnp.maximum(lo1, lo2))
        gscore.append(top1 + top2)
    best, gidx = gscore[0], jnp.zeros(gscore[0].shape, I32)
    for g in range(1, N_GROUPS):
        upd = gscore[g] > best
        best = jnp.where(upd, gscore[g], best)
        gidx = jnp.where(upd, g, gidx)

    def pick(rows, j):
        out = rows[j]
        for g in range(1, N_GROUPS):
            out = jnp.where(gidx == g, rows[per * g + j], out)
        return out

    vals = [pick(sb, j) for j in range(per)]
    raws = [pick(s, j) for j in range(per)]
    b1, i1, s1 = vals[0], jnp.zeros(gidx.shape, I32), raws[0]
    for j in range(1, per):
        upd = vals[j] > b1
        b1 = jnp.where(upd, vals[j], b1)
        i1 = jnp.where(upd, j, i1)
        s1 = jnp.where(upd, raws[j], s1)
    b2, i2, s2 = jnp.zeros_like(b1), jnp.full(gidx.shape, -1, I32), jnp.zeros_like(b1)
    for j in range(per):
        upd = (i1 != j) & ((i2 < 0) | (vals[j] > b2))
        b2 = jnp.where(upd, vals[j], b2)
        i2 = jnp.where(upd, j, i2)
        s2 = jnp.where(upd, raws[j], s2)
    denom = s1 + s2
    return per * gidx + i1, per * gidx + i2, s1 / denom, s2 / denom


def _post_kernel(x_ref, mix_ref, mq_ref, kvm_ref, wo_ref, g_ref, b_ref, rw_ref, rb_ref,
                 x1_ref, esel_ref, gate_ref):
    mq = mq_ref[...]
    kvm = kvm_ref[0]
    mem = []
    for h in range(MEM_HEADS):
        hs = slice(h * MEM_HEAD_DIM, (h + 1) * MEM_HEAD_DIM)
        vs = slice(MEM_WIDTH + h * MEM_HEAD_DIM, MEM_WIDTH + (h + 1) * MEM_HEAD_DIM)
        s = lax.dot_general(mq[:, hs], kvm[:, hs], _NT, preferred_element_type=F32) * (MEM_HEAD_DIM ** -0.5)
        p = jnp.exp(s - jnp.max(s, axis=-1, keepdims=True))
        p = p / jnp.sum(p, axis=-1, keepdims=True)
        mem.append(jnp.dot(p.astype(BF16), kvm[:, vs], preferred_element_type=F32))
    y = jnp.dot(mix_ref[...], wo_ref[0:D_MODEL, :], preferred_element_type=F32)
    for h in range(MEM_HEADS):
        r0 = D_MODEL + h * MEM_HEAD_DIM
        y = y + jnp.dot(mem[h].astype(BF16), wo_ref[r0:r0 + MEM_HEAD_DIM, :], preferred_element_type=F32)
    x1 = _layer_norm(DEEPNORM_ALPHA * x_ref[...] + y, g_ref[...], b_ref[...])
    x1_ref[...] = x1
    logits = lax.dot_general(rw_ref[...], x1.astype(BF16), _NT, preferred_element_type=F32)
    s_all = _sigmoid(logits)
    sb_all = s_all + rb_ref[...]
    e1, e2, w1, w2 = _route([sb_all[e:e + 1, :] for e in range(N_EXPERTS)],
                            [s_all[e:e + 1, :] for e in range(N_EXPERTS)])
    esel_ref[0:1, :] = e1
    esel_ref[1:2, :] = e2
    gate_ref[0:1, :] = w1
    gate_ref[1:2, :] = w2


def _post_mixer(x, mix, mq, kvm, w_o, ln_g, ln_b, rw_t, rbias, seq, tm=512):
    t = x.shape[0]
    tm = min(tm, seq)
    per_seq = seq // tm
    row = lambda width: pl.BlockSpec((tm, width), lambda i: (i, 0))
    full = lambda a: pl.BlockSpec(a.shape, lambda i: (0,) * a.ndim)
    g = ln_g.reshape(1, -1)
    b = ln_b.reshape(1, -1)
    return pl.pallas_call(
        _post_kernel,
        grid=(t // tm,),
        in_specs=[row(D_MODEL), row(D_MODEL), row(MEM_WIDTH),
                  pl.BlockSpec((1,) + kvm.shape[1:], lambda i: (i // per_seq, 0, 0)),
                  full(w_o), full(g), full(b), full(rw_t), full(rbias)],
        out_specs=[row(D_MODEL), pl.BlockSpec((TOPK_EXPERTS, tm), lambda i: (0, i)),
                   pl.BlockSpec((TOPK_EXPERTS, tm), lambda i: (0, i))],
        out_shape=[jax.ShapeDtypeStruct((t, D_MODEL), F32), jax.ShapeDtypeStruct((TOPK_EXPERTS, t), I32),
                   jax.ShapeDtypeStruct((TOPK_EXPERTS, t), F32)],
        compiler_params=_cparams(("parallel",)),
        name="post_mixer",
    )(x, mix, mq, kvm, w_o, g, b, rw_t, rbias)


def _moe_rank_kernel(e_ref, rank_ref, cnt_ref, base_ref, *, tr):
    i = pl.program_id(0)

    @pl.when(i == 0)
    def _():
        base_ref[...] = jnp.zeros(base_ref.shape, F32)

    e1 = e_ref[0:1, :]
    e2 = e_ref[1:2, :]
    row = lax.broadcasted_iota(I32, (N_EXPERTS, tr), 0)
    onehot = jnp.where((row == e1) | (row == e2), 1.0, 0.0)
    before = (lax.broadcasted_iota(I32, (tr, tr), 0) < lax.broadcasted_iota(I32, (tr, tr), 1)).astype(BF16)
    seen = jnp.dot(onehot.astype(BF16), before, preferred_element_type=F32) + base_ref[:, 0:1]
    rank_ref[0:1, :] = jnp.sum(jnp.where(row == e1, seen, 0.0), axis=0, keepdims=True).astype(I32)
    rank_ref[1:2, :] = jnp.sum(jnp.where(row == e2, seen, 0.0), axis=0, keepdims=True).astype(I32)
    base_ref[...] = base_ref[...] + jnp.sum(onehot, axis=1, keepdims=True)
    cnt_ref[...] = base_ref[...]


def _moe_rank(esel, tr=512):
    t = esel.shape[1]
    tr = min(tr, t)
    return pl.pallas_call(
        functools.partial(_moe_rank_kernel, tr=tr),
        grid=(t // tr,),
        in_specs=[pl.BlockSpec((TOPK_EXPERTS, tr), lambda i: (0, i))],
        out_specs=[pl.BlockSpec((TOPK_EXPERTS, tr), lambda i: (0, i)),
                   pl.BlockSpec((N_EXPERTS, LANES), lambda i: (0, 0))],
        out_shape=[jax.ShapeDtypeStruct((TOPK_EXPERTS, t), I32), jax.ShapeDtypeStruct((N_EXPERTS, LANES), F32)],
        scratch_shapes=[pltpu.VMEM((N_EXPERTS, LANES), F32)],
        compiler_params=_cparams(("arbitrary",)),
        name="moe_rank",
    )(esel)


def _moe_scatter_kernel(zero_ref, slot_ref, x_ref, buf_ref, zeros_ref, sem, *, ts, n_chunks):
    i = pl.program_id(0)

    def row_copy(tok, k):
        return pltpu.make_async_copy(x_ref.at[pl.ds(tok, 1)], buf_ref.at[pl.ds(slot_ref[0, k, tok], 1)], sem)

    @pl.when(i == 0)
    def _():
        zeros_ref[...] = jnp.zeros(zeros_ref.shape, F32)

        def fill(c, _):
            @pl.when(zero_ref[c] != 0)
            def _():
                cp = pltpu.make_async_copy(
                    zeros_ref, buf_ref.at[pl.ds(pl.multiple_of(c * MOE_CHUNK, MOE_CHUNK), MOE_CHUNK)], sem)
                cp.start()
                cp.wait()
            return 0

        lax.fori_loop(0, n_chunks, fill, 0)

    def start(tok, _):
        for k in range(TOPK_EXPERTS):
            row_copy(tok, k).start()
        return 0

    def wait(tok, _):
        for k in range(TOPK_EXPERTS):
            row_copy(tok, k).wait()
        return 0

    lax.fori_loop(0, ts, start, 0, unroll=8)
    lax.fori_loop(0, ts, wait, 0, unroll=8)


def _moe_scatter(x1, slots, zero_flags, n_chunks, ts=512):
    t = x1.shape[0]
    grid_spec = pltpu.PrefetchScalarGridSpec(
        num_scalar_prefetch=1,
        grid=(t // ts,),
        in_specs=[pl.BlockSpec((1, TOPK_EXPERTS, ts), lambda i, z: (i, 0, 0), memory_space=pltpu.SMEM),
                  pl.BlockSpec((ts, D_MODEL), lambda i, z: (i, 0))],
        out_specs=pl.BlockSpec(memory_space=pl.ANY),
        scratch_shapes=[pltpu.VMEM((MOE_CHUNK, D_MODEL), F32), pltpu.SemaphoreType.DMA(())],
    )
    return pl.pallas_call(
        functools.partial(_moe_scatter_kernel, ts=ts, n_chunks=n_chunks),
        grid_spec=grid_spec,
        out_shape=jax.ShapeDtypeStruct((n_chunks * MOE_CHUNK, D_MODEL), F32),
        compiler_params=_cparams(("arbitrary",)),
        name="moe_scatter",
    )(zero_flags, slots, x1)


def _moe_ffn_kernel(cexp_ref, nused_ref, x_ref, win_ref, wout_ref, y_ref):
    c = pl.program_id(0)

    @pl.when(c < nused_ref[0])
    def _():
        h = jnp.dot(x_ref[...].astype(BF16), win_ref[0], preferred_element_type=F32)
        gate_in = h[:, :EXPERT_FF]
        act = gate_in * _sigmoid(gate_in) * h[:, EXPERT_FF:]
        y_ref[...] = jnp.dot(act.astype(BF16), wout_ref[0], preferred_element_type=F32)

    @pl.when(c >= nused_ref[0])
    def _():
        y_ref[...] = jnp.zeros(y_ref.shape, F32)


def _moe_ffn(buf, chunk_exp, n_used, w_in, w_out):
    n_chunks = chunk_exp.shape[0]
    grid_spec = pltpu.PrefetchScalarGridSpec(
        num_scalar_prefetch=2,
        grid=(n_chunks,),
        in_specs=[pl.BlockSpec((MOE_CHUNK, D_MODEL), lambda c, ce, nu: (c, 0)),
                  pl.BlockSpec((1, D_MODEL, 2 * EXPERT_FF), lambda c, ce, nu: (ce[c], 0, 0)),
                  pl.BlockSpec((1, EXPERT_FF, D_MODEL), lambda c, ce, nu: (ce[c], 0, 0))],
        out_specs=pl.BlockSpec((MOE_CHUNK, D_MODEL), lambda c, ce, nu: (c, 0)),
    )
    return pl.pallas_call(
        _moe_ffn_kernel,
        grid_spec=grid_spec,
        out_shape=jax.ShapeDtypeStruct(buf.shape, F32),
        compiler_params=_cparams(("arbitrary",)),
        name="moe_ffn",
    )(chunk_exp, n_used, buf, w_in, w_out)


def _moe_combine_kernel(slot_ref, x_ref, gate_ref, g_ref, b_ref, yb_ref, o_ref, rows_ref, sem, *, tc):
    def row_copy(tok, k):
        return pltpu.make_async_copy(yb_ref.at[pl.ds(slot_ref[0, k, tok], 1)], rows_ref.at[k, pl.ds(tok, 1)], sem)

    def start(tok, _):
        for k in range(TOPK_EXPERTS):
            row_copy(tok, k).start()
        return 0

    def wait(tok, _):
        for k in range(TOPK_EXPERTS):
            row_copy(tok, k).wait()
        return 0

    lax.fori_loop(0, tc, start, 0, unroll=8)
    lax.fori_loop(0, tc, wait, 0, unroll=8)
    gates = gate_ref[...]
    y = rows_ref[0] * gates[:, 0:1] + rows_ref[1] * gates[:, 1:2]
    o_ref[...] = _layer_norm(DEEPNORM_ALPHA * x_ref[...] + y, g_ref[...], b_ref[...])


def _moe_combine(x1, gates, slots, yb, ln_g, ln_b, tc=512):
    t = x1.shape[0]
    g = ln_g.reshape(1, -1)
    b = ln_b.reshape(1, -1)
    vec = pl.BlockSpec((1, D_MODEL), lambda i: (0, 0))
    return pl.pallas_call(
        functools.partial(_moe_combine_kernel, tc=tc),
        grid=(t // tc,),
        in_specs=[pl.BlockSpec((1, TOPK_EXPERTS, tc), lambda i: (i, 0, 0), memory_space=pltpu.SMEM),
                  pl.BlockSpec((tc, D_MODEL), lambda i: (i, 0)),
                  pl.BlockSpec((tc, LANES), lambda i: (i, 0)),
                  vec, vec,
                  pl.BlockSpec(memory_space=pl.ANY)],
        out_specs=pl.BlockSpec((tc, D_MODEL), lambda i: (i, 0)),
        out_shape=jax.ShapeDtypeStruct((t, D_MODEL), F32),
        scratch_shapes=[pltpu.VMEM((TOPK_EXPERTS, tc, D_MODEL), F32), pltpu.SemaphoreType.DMA(())],
        compiler_params=_cparams(("arbitrary",)),
        name="moe_combine",
    )(slots, x1, gates, g, b, yb)


def _moe(x1, esel, gate_t, w_in, w_out, ln_g, ln_b, tile=512):
    t = x1.shape[0]
    tile = min(tile, t)
    n_chunks = -(-t * TOPK_EXPERTS // MOE_CHUNK) + N_EXPERTS
    rank, cnt = _moe_rank(esel)
    counts = cnt[:, 0].astype(I32)
    padded = (counts + MOE_CHUNK - 1) // MOE_CHUNK * MOE_CHUNK
    pad_end = jnp.cumsum(padded)
    pad_start = pad_end - padded
    expert = jnp.arange(N_EXPERTS, dtype=I32)
    slot = rank + jnp.sum(jnp.where(esel[None] == expert[:, None, None], pad_start[:, None, None], 0), axis=0)
    slots = slot.reshape(TOPK_EXPERTS, t // tile, tile).transpose(1, 0, 2)
    chunk_lo = jnp.arange(n_chunks, dtype=I32) * MOE_CHUNK
    chunk_exp = jnp.minimum(jnp.sum((pad_end[None, :] <= chunk_lo[:, None]).astype(I32), axis=1), N_EXPERTS - 1)
    n_used = (pad_end[-1] // MOE_CHUNK).astype(I32).reshape(1)
    is_last = jnp.any((chunk_lo[:, None] + MOE_CHUNK) == pad_end[None, :], axis=1)
    zero_flags = (is_last | (chunk_lo >= pad_end[-1])).astype(I32)
    buf = _moe_scatter(x1, slots, zero_flags, n_chunks, ts=tile)
    yb = _moe_ffn(buf, chunk_exp, n_used, w_in, w_out)
    gates = _pad_cols(gate_t.T, LANES)
    return _moe_combine(x1, gates, slots, yb, ln_g, ln_b, tc=tile)


def _pad_cols(w, width, offset=0):
    out = jnp.zeros((w.shape[0], width), w.dtype)
    return out.at[:, offset:offset + w.shape[1]].set(w)


def _rope_rows(half, lane_lo, lane_hi):
    inv_freq = ROPE_THETA ** (-jnp.arange(half, dtype=F32) / half)
    lane = jnp.arange(LANES)
    active = (lane >= lane_lo) & (lane < lane_hi)
    pos_in = (lane - lane_lo) % (2 * half)
    freq = jnp.where(active, inv_freq[pos_in % half], 0.0).astype(F32)
    sign = jnp.where(active, jnp.where(pos_in < half, -1.0, 1.0), 0.0).astype(F32)
    return freq.reshape(1, LANES), sign.reshape(1, LANES)


def kernel(x, mem, positions, router_w, router_bias, l0_w_in, l0_mem_wkv, l0_w_o, l0_ln_g, l0_ln_b, l0_moe_w_in, l0_moe_w_out, l1_w_in, l1_conv_w, l1_conv_b, l1_conv_norm_g, l1_conv_norm_b, l1_mem_wkv, l1_w_o, l1_ln_g, l1_ln_b, l1_moe_w_in, l1_moe_w_out, l2_w_in, l2_mem_wkv, l2_w_o, l2_ln_g, l2_ln_b, l2_moe_w_in, l2_moe_w_out, l3_w_in, l3_q_norm_g, l3_kv_norm_g, l3_w_q_up, l3_w_kv_up, l3_mem_wkv, l3_w_o, l3_ln_g, l3_ln_b, l3_moe_w_in, l3_moe_w_out):
    bsz, seq, d = x.shape
    t = bsz * seq
    n_mem = mem.shape[1]
    xt = x.reshape(t, d)
    mem2 = mem.reshape(bsz * n_mem, d)
    pos_b = jnp.broadcast_to(positions.astype(F32).reshape(t, 1), (t, LANES))
    rw_t = router_w.T.astype(BF16)
    rbias = router_bias.astype(F32).reshape(N_EXPERTS, 1)

    def finish_layer(xt, mix, mq, mem_wkv, w_o, ln_g, ln_b, moe_w_in, moe_w_out):
        (kvm,) = _proj(mem2, mem_wkv.astype(BF16), [(0, 2 * MEM_WIDTH, 2 * MEM_WIDTH, 0)], [BF16],
                       tm=n_mem, name="mem_kv")
        kvm = kvm.reshape(bsz, n_mem, 2 * MEM_WIDTH)
        x1, esel, gate_t = _post_mixer(xt, mix, mq, kvm, w_o.astype(BF16), ln_g[0], ln_b[0], rw_t, rbias, seq)
        return _moe(x1, esel, gate_t, moe_w_in.astype(BF16), moe_w_out.astype(BF16), ln_g[1], ln_b[1])

    w0 = l0_w_in.astype(BF16)
    n0 = w0.shape[1]
    (h0,) = _proj(xt, w0, [(0, n0, n0, 0)], [BF16], name="proj0")
    mix = _sb_attention(h0, bsz, seq)
    mq = h0[:, 3 * D_MODEL:]
    xt = finish_layer(xt, mix, mq, l0_mem_wkv, l0_w_o, l0_ln_g, l0_ln_b, l0_moe_w_in, l0_moe_w_out)

    w1 = l1_w_in.astype(BF16)
    a, g, mq = _proj(xt, w1, [(0, D_MODEL, D_MODEL, 0), (D_MODEL, D_MODEL, D_MODEL, 0),
                              (2 * D_MODEL, MEM_WIDTH, MEM_WIDTH, 0)], [F32, F32, BF16], name="proj1")
    mix = _conformer_conv(a, g, l1_conv_w, l1_conv_b, l1_conv_norm_g, l1_conv_norm_b, seq)
    xt = finish_layer(xt, mix, mq, l1_mem_wkv, l1_w_o, l1_ln_g, l1_ln_b, l1_moe_w_in, l1_moe_w_out)

    w2 = l2_w_in
    c = 0
    parts = []
    for width in (DSA_HEADS * DSA_HEAD_DIM, DSA_HEAD_DIM, DSA_HEAD_DIM, IDX_HEADS * IDX_DIM, IDX_DIM, IDX_HEADS,
                  MEM_WIDTH):
        parts.append(w2[:, c:c + width])
        c += width
    wq, wk, wv, wqi, wki, wwi, wmq = parts
    w2p = jnp.concatenate([wq, _pad_cols(wk, LANES), _pad_cols(wv, LANES), wqi, _pad_cols(wki, LANES),
                           _pad_cols(wwi, LANES), wmq], axis=1).astype(BF16)
    freq, sign = _rope_rows(DSA_HEAD_DIM // 2, 0, LANES)
    cos_a, sin_a = _rope_tables(pos_b, freq, sign)
    rh = DSA_HEAD_DIM // 2
    groups = [(0, 1024, 1024, rh), (1024, LANES, DSA_HEAD_DIM, rh), (1152, LANES, LANES, 0),
              (1280, 512, 512, rh), (1792, LANES, IDX_DIM, rh), (1920, LANES, LANES, 0),
              (2048, MEM_WIDTH, MEM_WIDTH, 0)]
    q, k, v, qi, ki, wi, mq = _proj(xt, w2p, groups, [BF16, BF16, BF16, BF16, BF16, F32, BF16],
                                    rope=(cos_a, sin_a), name="proj2")
    mix = _dsa_attention(q, qi, wi, k, v, ki, bsz, seq)
    xt = finish_layer(xt, mix, mq, l2_mem_wkv, l2_w_o, l2_ln_g, l2_ln_b, l2_moe_w_in, l2_moe_w_out)

    w3 = l3_w_in
    rope_lo = MLA_NOPE
    w_cq = w3[:, :MLA_Q_RANK]
    w_ckv = w3[:, MLA_Q_RANK:MLA_Q_RANK + MLA_KV_RANK]
    w_kr = w3[:, MLA_Q_RANK + MLA_KV_RANK:MLA_Q_RANK + MLA_KV_RANK + MLA_ROPE]
    w_mq = w3[:, MLA_Q_RANK + MLA_KV_RANK + MLA_ROPE:]
    w3p = jnp.concatenate([w_cq, w_ckv, _pad_cols(w_kr, LANES, rope_lo), w_mq], axis=1).astype(BF16)
    freq, sign = _rope_rows(MLA_ROPE // 2, rope_lo, rope_lo + MLA_ROPE)
    cos_b, sin_b = _rope_tables(pos_b, freq, sign)
    groups = [(0, MLA_Q_RANK, MLA_Q_RANK, 0), (MLA_Q_RANK, MLA_KV_RANK, MLA_KV_RANK, 0),
              (MLA_Q_RANK + MLA_KV_RANK, LANES, LANES, MLA_ROPE // 2),
              (MLA_Q_RANK + MLA_KV_RANK + LANES, MEM_WIDTH, MEM_WIDTH, 0)]
    c_q, c_kv, kr, mq = _proj(xt, w3p, groups, [F32, F32, F32, BF16], rope=(cos_b, sin_b), name="proj3")
    qk_dim = MLA_NOPE + MLA_ROPE
    wq_heads = l3_w_q_up.reshape(MLA_Q_RANK, MLA_HEADS, qk_dim)
    wq_p = jnp.zeros((MLA_Q_RANK, MLA_HEADS, LANES), F32).at[:, :, :qk_dim].set(wq_heads)
    wkv_heads = l3_w_kv_up.reshape(MLA_KV_RANK, MLA_HEADS, MLA_NOPE + MLA_V)
    wk_p = jnp.zeros((MLA_KV_RANK, MLA_HEADS, LANES), F32).at[:, :, :MLA_NOPE].set(wkv_heads[:, :, :MLA_NOPE])
    wv_p = wkv_heads[:, :, MLA_NOPE:]
    flat = lambda w: w.reshape(w.shape[0], MLA_HEADS * LANES).astype(BF16)
    qh, kh, vh = _mla_prep(c_q, c_kv, kr, cos_b, sin_b, l3_q_norm_g, l3_kv_norm_g, flat(wq_p), flat(wk_p),
                           flat(wv_p))
    mix = _causal_attention(qh, kh, vh, bsz, seq, MLA_HEADS, qk_dim ** -0.5)
    xt = finish_layer(xt, mix, mq, l3_mem_wkv, l3_w_o, l3_ln_g, l3_ln_b, l3_moe_w_in, l3_moe_w_out)
    return xt.reshape(bsz, seq, d)
```

```python
import functools

import jax
import jax.numpy as jnp
from jax import lax
from jax.experimental import pallas as pl
from jax.experimental.pallas import tpu as pltpu

F32 = jnp.float32
BF16 = jnp.bfloat16
I32 = jnp.int32

D_MODEL = 1024
DEPTH = 4
ROPE_THETA = 10000.0
LN_EPS = 1e-5
RMS_EPS = 1e-6
DEEPNORM_ALPHA = (2.0 * DEPTH) ** 0.25
SB_HEADS = 8
CONV_WIDTH = 31
DSA_HEADS = 16
DSA_HEAD_DIM = 64
IDX_HEADS = 8
IDX_DIM = 64
IDX_TOPK_MAX = 256
MLA_HEADS = 8
MLA_Q_RANK = 384
MLA_KV_RANK = 256
MLA_NOPE = 64
MLA_ROPE = 32
MLA_V = 128
MEM_HEADS = 4
MEM_HEAD_DIM = 64
MEM_WIDTH = 256
N_EXPERTS = 16
N_GROUPS = 4
EXPERT_FF = 512
TOPK_EXPERTS = 2
MOE_CHUNK = 512

LANES = 128
VMEM_LIMIT_BYTES = 56 * 1024 * 1024
NEG_BIG = -1e30
MAX_SAFE_SHIFT = 40.0
INT_MIN = -2 ** 31

_NT = (((1,), (1,)), ((), ()))


def _cparams(sem):
    return pltpu.CompilerParams(dimension_semantics=sem, vmem_limit_bytes=VMEM_LIMIT_BYTES)


def _layer_norm(v, g, b):
    mu = jnp.mean(v, axis=-1, keepdims=True)
    d = v - mu
    var = jnp.mean(d * d, axis=-1, keepdims=True)
    return d * lax.rsqrt(var + LN_EPS) * g + b


def _sigmoid(v):
    return 1.0 / (1.0 + jnp.exp(-v))


def _rope_table_kernel(pos_ref, freq_ref, sign_ref, cos_ref, sin_ref):
    ang = pos_ref[...] * freq_ref[...]
    cos_ref[...] = jnp.cos(ang)
    sin_ref[...] = jnp.sin(ang) * sign_ref[...]


def _rope_tables(pos_b, freq_row, sign_row, tm=512):
    t = pos_b.shape[0]
    row = pl.BlockSpec((tm, LANES), lambda i: (i, 0))
    one = pl.BlockSpec((1, LANES), lambda i: (0, 0))
    return pl.pallas_call(
        _rope_table_kernel,
        grid=(t // tm,),
        in_specs=[row, one, one],
        out_specs=[row, row],
        out_shape=[jax.ShapeDtypeStruct((t, LANES), F32)] * 2,
        compiler_params=_cparams(("parallel",)),
        name="rope_tables",
    )(pos_b, freq_row, sign_row)


def _rope128(v, cos, sin, half):
    lane = lax.broadcasted_iota(I32, v.shape, 1)
    first = (lane & (2 * half - 1)) < half
    fwd = pltpu.roll(v, LANES - half, 1)
    bwd = pltpu.roll(v, half, 1)
    return v * cos + jnp.where(first, fwd, bwd) * sin


def _proj_kernel(*refs, groups, has_rope):
    x_ref, w_ref = refs[0], refs[1]
    nin = 4 if has_rope else 2
    outs = refs[nin:]
    x = x_ref[...].astype(BF16)
    for (start, width, out_width, rope_half), o_ref in zip(groups, outs):
        acc = jnp.dot(x, w_ref[:, start:start + width], preferred_element_type=F32)
        if rope_half:
            cos = refs[2][...]
            sin = refs[3][...]
            for c in range(width // LANES):
                r = _rope128(acc[:, c * LANES:(c + 1) * LANES], cos, sin, rope_half)
                lo = c * LANES
                hi = min(lo + LANES, out_width)
                o_ref[:, lo:hi] = r[:, :hi - lo].astype(o_ref.dtype)
        else:
            o_ref[...] = acc[:, :out_width].astype(o_ref.dtype)


def _proj(x, w, groups, out_dtypes, rope=None, tm=512, name="proj"):
    m, k = x.shape
    n = w.shape[1]
    tm = min(tm, m)
    in_specs = [pl.BlockSpec((tm, k), lambda i: (i, 0)), pl.BlockSpec((k, n), lambda i: (0, 0))]
    args = [x, w]
    if rope is not None:
        in_specs += [pl.BlockSpec((tm, LANES), lambda i: (i, 0))] * 2
        args += list(rope)
    out_specs = [pl.BlockSpec((tm, g[2]), lambda i: (i, 0)) for g in groups]
    out_shape = [jax.ShapeDtypeStruct((m, g[2]), dt) for g, dt in zip(groups, out_dtypes)]
    return pl.pallas_call(
        functools.partial(_proj_kernel, groups=tuple(groups), has_rope=rope is not None),
        grid=(m // tm,),
        in_specs=in_specs,
        out_specs=out_specs,
        out_shape=out_shape,
        compiler_params=_cparams(("parallel",)),
        name=name,
    )(*args)


def _sb_kernel(q_ref, k_ref, v_ref, o_ref, *, tq, ck, nh):
    i = pl.program_id(2)
    r = lax.broadcasted_iota(I32, (ck, ck), 0)
    c = lax.broadcasted_iota(I32, (ck, ck), 1)
    suffix = (r >= c).astype(BF16)
    row = lax.broadcasted_iota(I32, (tq, ck), 0)
    col = lax.broadcasted_iota(I32, (tq, ck), 1)

    def head_block(hh, start, carry, diag):
        tail, acc = carry
        hs = slice(hh * LANES, (hh + 1) * LANES)
        kb = k_ref[pl.ds(start, tq), hs]
        z = lax.dot_general(q_ref[:, hs], kb, _NT, preferred_element_type=F32)
        for cc in reversed(range(tq // ck)):
            zc = z[:, cc * ck:(cc + 1) * ck]
            sp = jnp.maximum(zc, 0.0) + jnp.log(1.0 + jnp.exp(-jnp.abs(zc)))
            if diag:
                strict = col + cc * ck < row
                sp = jnp.where(strict, sp, 0.0)
            cum = jnp.dot(sp.astype(BF16), suffix, preferred_element_type=F32)
            a = jnp.exp(zc - cum - tail)
            if diag:
                a = jnp.where(strict, a, 0.0)
            vb = v_ref[pl.ds(pl.multiple_of(start + cc * ck, ck), ck), hs]
            acc = acc + jnp.dot(a.astype(BF16), vb, preferred_element_type=F32)
            tail = tail + jnp.sum(sp, axis=-1, keepdims=True)
        return tail, acc

    def block(j, carries, diag):
        start = pl.multiple_of(j * tq, tq)
        return tuple(head_block(hh, start, carries[hh], diag) for hh in range(nh))

    carries = tuple((jnp.zeros((tq, 1), F32), jnp.zeros((tq, LANES), F32)) for _ in range(nh))
    carries = block(i, carries, True)
    carries = lax.fori_loop(0, i, lambda jj, c: block(i - 1 - jj, c, False), carries)
    for hh in range(nh):
        o_ref[:, hh * LANES:(hh + 1) * LANES] = carries[hh][1].astype(o_ref.dtype)


def _sb_attention(h, bsz, seq, tq=512, ck=256, nh=2):
    tq = min(tq, seq)
    ck = min(ck, tq)
    nq = seq // tq
    t = bsz * seq
    hg = SB_HEADS // nh
    wide = nh * LANES
    return pl.pallas_call(
        functools.partial(_sb_kernel, tq=tq, ck=ck, nh=nh),
        grid=(bsz, hg, nq),
        in_specs=[
            pl.BlockSpec((tq, wide), lambda b, hh, i: (b * nq + i, hh)),
            pl.BlockSpec((seq, wide), lambda b, hh, i: (b, hg + hh)),
            pl.BlockSpec((seq, wide), lambda b, hh, i: (b, 2 * hg + hh)),
        ],
        out_specs=pl.BlockSpec((tq, wide), lambda b, hh, i: (b * nq + i, hh)),
        out_shape=jax.ShapeDtypeStruct((t, SB_HEADS * LANES), BF16),
        compiler_params=_cparams(("parallel", "parallel", "arbitrary")),
        name="sb_attention",
    )(h, h, h)


def _conv_kernel(a_ref, g_ref, ap_ref, gp_ref, w_ref, cb_ref, ng_ref, nb_ref, o_ref, hext_ref, y_ref,
                 *, tm, halo, tiles_per_seq):
    i = pl.program_id(0)
    hext_ref[halo:halo + tm, :] = a_ref[...] * _sigmoid(g_ref[...])
    prev = ap_ref[...] * _sigmoid(gp_ref[...])
    first = (i % tiles_per_seq) == 0
    hext_ref[0:halo, :] = jnp.where(first, 0.0, prev)
    off = halo - (CONV_WIDTH - 1)
    for c in range(D_MODEL // LANES):
        cs = slice(c * LANES, (c + 1) * LANES)
        acc = jnp.zeros((tm, LANES), F32)
        for j in range(CONV_WIDTH):
            acc = acc + w_ref[j:j + 1, cs] * hext_ref[off + j:off + j + tm, cs]
        y_ref[:, cs] = acc
    y = y_ref[...] + cb_ref[...]
    y = _layer_norm(y, ng_ref[...], nb_ref[...])
    o_ref[...] = (y * _sigmoid(y)).astype(o_ref.dtype)


def _conformer_conv(a, g, conv_w, conv_b, norm_g, norm_b, seq, tm=256, halo=32):
    t = a.shape[0]
    tm = min(tm, seq)
    r = tm // halo
    cur = pl.BlockSpec((tm, D_MODEL), lambda i: (i, 0))
    prv = pl.BlockSpec((halo, D_MODEL), lambda i: (jnp.maximum(i * r - 1, 0), 0))
    vec = pl.BlockSpec((1, D_MODEL), lambda i: (0, 0))
    return pl.pallas_call(
        functools.partial(_conv_kernel, tm=tm, halo=halo, tiles_per_seq=seq // tm),
        grid=(t // tm,),
        in_specs=[cur, cur, prv, prv, pl.BlockSpec((CONV_WIDTH, D_MODEL), lambda i: (0, 0)), vec, vec, vec],
        out_specs=cur,
        out_shape=jax.ShapeDtypeStruct((t, D_MODEL), BF16),
        scratch_shapes=[pltpu.VMEM((tm + halo, D_MODEL), F32), pltpu.VMEM((tm, D_MODEL), F32)],
        compiler_params=_cparams(("parallel",)),
        name="conformer_conv",
    )(a, g, a, g, conv_w, conv_b.reshape(1, -1), norm_g.reshape(1, -1), norm_b.reshape(1, -1))


def _dsa_kernel(q_ref, qi_ref, wi_ref, k_ref, v_ref, ki_ref, o_ref,
                qs_ref, qis_ref, keys_ref, eq_hi_ref, mrun_ref, mb_ref, acc_ref, kmax2_ref,
                *, tq, tkb, topk, seq, idx_bits, scale):
    i = pl.program_id(1)
    nkb = (i * tq + tq + tkb - 1) // tkb

    @pl.when(i == 0)
    def _():
        rows = min(tkb, seq)

        def body(c, m):
            kk = k_ref[pl.ds(pl.multiple_of(c * rows, rows), rows), :].astype(F32)
            return jnp.maximum(m, jnp.sum(kk * kk, axis=-1, keepdims=True))

        kmax2_ref[0] = jnp.max(lax.fori_loop(0, seq // rows, body, jnp.zeros((rows, 1), F32)))

    for h in range(DSA_HEADS):
        qs_ref[h * tq:(h + 1) * tq, :] = q_ref[:, h * DSA_HEAD_DIM:(h + 1) * DSA_HEAD_DIM]
    for h in range(IDX_HEADS):
        qis_ref[h * tq:(h + 1) * tq, :] = qi_ref[:, h * IDX_DIM:(h + 1) * IDX_DIM]
    wi = wi_ref[...] * (IDX_HEADS ** -0.5 * IDX_DIM ** -0.5)
    wcols = [wi[:, h:h + 1] for h in range(IDX_HEADS)]
    q_pos = i * tq + lax.broadcasted_iota(I32, (tq, 1), 0)
    col0 = lax.broadcasted_iota(I32, (tq, tkb), 1)

    def scores(jb, _):
        start = pl.multiple_of(jb * tkb, tkb)
        kib = ki_ref[pl.ds(start, tkb), :]
        dots = lax.dot_general(qis_ref[...], kib, _NT, preferred_element_type=F32)
        d3 = dots.reshape(IDX_HEADS, tq, tkb)
        sc = wcols[0] * jnp.maximum(d3[0], 0.0)
        for h in range(1, IDX_HEADS):
            sc = sc + wcols[h] * jnp.maximum(d3[h], 0.0)
        sc = sc + 0.0
        bits = lax.bitcast_convert_type(sc, I32)
        key = bits ^ ((bits >> 31) & 0x7FFFFFFF)
        key = jnp.where(col0 + start <= q_pos, key, INT_MIN)
        keys_ref[:, pl.ds(start, tkb)] = key
        return 0

    lax.fori_loop(0, nkb, scores, 0)

    def count_if(pred):
        def body(c, acc):
            start = pl.multiple_of(c * tkb, tkb)
            ind = jnp.where(pred(keys_ref[:, pl.ds(start, tkb)], start), 1.0, 0.0)
            for s in range(tkb // LANES):
                acc = acc + ind[:, s * LANES:(s + 1) * LANES]
            return acc
        acc = lax.fori_loop(0, nkb, body, jnp.zeros((tq, LANES), F32))
        return jnp.sum(acc, axis=-1, keepdims=True)

    kf = float(topk)
    short = count_if(lambda kb, st: kb > INT_MIN) < kf
    n_pos = count_if(lambda kb, st: kb >= 0)
    tau = jnp.where(n_pos >= kf, 0, INT_MIN).astype(I32)
    n_ge = jnp.where(n_pos >= kf, n_pos, 2.0 * seq)

    def unsettled(n_ge):
        return jnp.max(jnp.where(short | (n_ge == kf), 0.0, 1.0))

    def bisect(state):
        it, tau, n_ge, _ = state
        cand = tau + jnp.left_shift(jnp.int32(1), 30 - it)
        n_cand = count_if(lambda kb, st: kb >= cand)
        keep = n_cand >= kf
        tau = jnp.where(keep, cand, tau)
        n_ge = jnp.where(keep, n_cand, n_ge)
        return it + 1, tau, n_ge, unsettled(n_ge)

    _, tau, n_ge, _ = lax.while_loop(lambda st: (st[0] < 31) & (st[3] > 0.0), bisect,
                                     (jnp.int32(0), tau, n_ge, unsettled(n_ge)))
    n_gt = count_if(lambda kb, st: kb > tau)
    need = kf - n_gt
    eq_hi_ref[...] = jnp.where(short, -1, seq).astype(I32)
    tied = jnp.max(jnp.where((n_ge > kf) & jnp.logical_not(short), 1.0, 0.0)) > 0.0

    @pl.when(tied)
    def _():
        def step(it, p):
            cand = p + jnp.left_shift(jnp.int32(1), idx_bits - it)
            cnt = count_if(lambda kb, st: (kb == tau) & (col0 + st < cand))
            return jnp.where((cand <= seq) & (cnt < need), cand, p)

        p = lax.fori_loop(0, idx_bits + 1, step, jnp.zeros((tq, 1), I32))
        eq_hi_ref[...] = jnp.where(short, -1, p)

    eq_hi = eq_hi_ref[...]

    def selected(start):
        kb = keys_ref[:, pl.ds(start, tkb)]
        return (kb > tau) | ((kb == tau) & (col0 + start <= eq_hi))

    qf = qs_ref[...].astype(F32)
    q_norm2 = jnp.sum(qf * qf, axis=-1, keepdims=True).reshape(DSA_HEADS, tq, 1)
    bound = jnp.sqrt(q_norm2 * kmax2_ref[0]) * scale
    bound_ok = jnp.max(bound) < MAX_SAFE_SHIFT

    @pl.when(bound_ok)
    def _():
        mb_ref[...] = jnp.broadcast_to(bound, mb_ref.shape)

    @pl.when(jnp.logical_not(bound_ok))
    def _():
        mrun_ref[...] = jnp.full(mrun_ref.shape, NEG_BIG, F32)

        def row_max(jb, _):
            start = pl.multiple_of(jb * tkb, tkb)
            s = lax.dot_general(qs_ref[...], k_ref[pl.ds(start, tkb), :], _NT, preferred_element_type=F32)
            s3 = jnp.where(selected(start)[None], s.reshape(DSA_HEADS, tq, tkb), NEG_BIG)
            m = mrun_ref[...]
            for c in range(tkb // LANES):
                m = jnp.maximum(m, s3[:, :, c * LANES:(c + 1) * LANES])
            mrun_ref[...] = m
            return 0

        lax.fori_loop(0, nkb, row_max, 0)
        m_row = jnp.max(mrun_ref[...], axis=-1, keepdims=True) * scale
        mb_ref[...] = jnp.broadcast_to(m_row, mb_ref.shape)

    acc_ref[...] = jnp.zeros(acc_ref.shape, F32)
    vlane = lax.broadcasted_iota(I32, (tkb, LANES), 1)

    def attend(jb, _):
        start = pl.multiple_of(jb * tkb, tkb)
        kb = k_ref[pl.ds(start, tkb), :]
        vb = jnp.where(vlane < DSA_HEAD_DIM, v_ref[pl.ds(start, tkb), :], 1.0).astype(BF16)
        s = lax.dot_general(qs_ref[...], kb, _NT, preferred_element_type=F32)
        t3 = s.reshape(DSA_HEADS, tq, tkb) * scale - mb_ref[...]
        p = jnp.where(selected(start)[None], jnp.exp(t3), 0.0)
        acc_ref[...] += jnp.dot(p.reshape(DSA_HEADS * tq, tkb).astype(BF16), vb, preferred_element_type=F32)
        return 0

    lax.fori_loop(0, nkb, attend, 0)
    acc = acc_ref[...]
    out = acc / pltpu.roll(acc, LANES - DSA_HEAD_DIM, 1)
    for h in range(DSA_HEADS):
        o_ref[:, h * DSA_HEAD_DIM:(h + 1) * DSA_HEAD_DIM] = (
            out[h * tq:(h + 1) * tq, :DSA_HEAD_DIM].astype(o_ref.dtype))


def _dsa_attention(q, qi, wi, k, v, ki, bsz, seq, tq=128, tkb=512):
    t = bsz * seq
    tq = min(tq, seq)
    tkb = min(tkb, seq)
    nq = seq // tq
    topk = min(IDX_TOPK_MAX, seq // 4)
    idx_bits = max(seq - 1, 1).bit_length()
    qrow = lambda width: pl.BlockSpec((tq, width), lambda b, i: (b * nq + i, 0))
    kv = lambda width: pl.BlockSpec((seq, width), lambda b, i: (b, 0), pipeline_mode=pl.Buffered(1))
    return pl.pallas_call(
        functools.partial(_dsa_kernel, tq=tq, tkb=tkb, topk=topk, seq=seq, idx_bits=idx_bits,
                          scale=DSA_HEAD_DIM ** -0.5),
        grid=(bsz, nq),
        in_specs=[qrow(DSA_HEADS * DSA_HEAD_DIM), qrow(IDX_HEADS * IDX_DIM), qrow(LANES),
                  kv(DSA_HEAD_DIM), kv(LANES), kv(IDX_DIM)],
        out_specs=qrow(DSA_HEADS * DSA_HEAD_DIM),
        out_shape=jax.ShapeDtypeStruct((t, DSA_HEADS * DSA_HEAD_DIM), BF16),
        scratch_shapes=[
            pltpu.VMEM((DSA_HEADS * tq, DSA_HEAD_DIM), BF16),
            pltpu.VMEM((IDX_HEADS * tq, IDX_DIM), BF16),
            pltpu.VMEM((tq, seq), I32),
            pltpu.VMEM((tq, 1), I32),
            pltpu.VMEM((DSA_HEADS, tq, LANES), F32),
            pltpu.VMEM((DSA_HEADS, tq, tkb), F32),
            pltpu.VMEM((DSA_HEADS * tq, LANES), F32),
            pltpu.SMEM((1,), F32),
        ],
        compiler_params=_cparams(("parallel", "arbitrary")),
        name="dsa_attention",
    )(q, qi, wi, k, v, ki)


def _mla_prep_kernel(cq_ref, ckv_ref, kr_ref, cos_ref, sin_ref, qg_ref, kvg_ref, wq_ref, wk_ref, wv_ref,
                     q_ref, k_ref, v_ref):
    def rms(c, g):
        return c * lax.rsqrt(jnp.mean(c * c, axis=-1, keepdims=True) + RMS_EPS) * g

    cos = cos_ref[...]
    sin = sin_ref[...]
    qn = rms(cq_ref[...], qg_ref[...]).astype(BF16)
    kvn = rms(ckv_ref[...], kvg_ref[...]).astype(BF16)
    kr = kr_ref[...]
    for h in range(MLA_HEADS):
        hs = slice(h * LANES, (h + 1) * LANES)
        qh = jnp.dot(qn, wq_ref[:, hs], preferred_element_type=F32)
        q_ref[:, hs] = _rope128(qh, cos, sin, MLA_ROPE // 2).astype(q_ref.dtype)
        kh = jnp.dot(kvn, wk_ref[:, hs], preferred_element_type=F32)
        k_ref[:, hs] = (kh + kr).astype(k_ref.dtype)
        v_ref[:, hs] = jnp.dot(kvn, wv_ref[:, hs], preferred_element_type=F32).astype(v_ref.dtype)


def _mla_prep(c_q, c_kv, kr, cos, sin, q_norm_g, kv_norm_g, wq, wk, wv, tm=512):
    t = c_q.shape[0]
    tm = min(tm, t)
    row = lambda width: pl.BlockSpec((tm, width), lambda i: (i, 0))
    full = lambda a: pl.BlockSpec(a.shape, lambda i: (0, 0))
    qg = q_norm_g.reshape(1, -1)
    kvg = kv_norm_g.reshape(1, -1)
    width = MLA_HEADS * LANES
    return pl.pallas_call(
        _mla_prep_kernel,
        grid=(t // tm,),
        in_specs=[row(MLA_Q_RANK), row(MLA_KV_RANK), row(LANES), row(LANES), row(LANES),
                  full(qg), full(kvg), full(wq), full(wk), full(wv)],
        out_specs=[row(width)] * 3,
        out_shape=[jax.ShapeDtypeStruct((t, width), BF16)] * 3,
        compiler_params=_cparams(("parallel",)),
        name="mla_prep",
    )(c_q, c_kv, kr, cos, sin, qg, kvg, wq, wk, wv)


def _flash_kernel(q_ref, k_ref, v_ref, o_ref, *, tq, nh, scale):
    i = pl.program_id(2)
    row = lax.broadcasted_iota(I32, (tq, tq), 0)
    col = lax.broadcasted_iota(I32, (tq, tq), 1)
    causal = col <= row
    ones = jnp.ones((tq, LANES), BF16)

    def head_block(hh, start, carry, diag):
        m, acc = carry
        hs = slice(hh * LANES, (hh + 1) * LANES)
        kb = k_ref[pl.ds(start, tq), hs]
        vb = jnp.concatenate([v_ref[pl.ds(start, tq), hs], ones], axis=1)
        s = lax.dot_general(q_ref[:, hs], kb, _NT, preferred_element_type=F32) * scale
        if diag:
            s = jnp.where(causal, s, NEG_BIG)
        m_new = jnp.maximum(m, jnp.max(s, axis=-1, keepdims=True))
        alpha = jnp.exp(m - m_new)
        p = jnp.exp(s - m_new)
        acc = alpha * acc + jnp.dot(p.astype(BF16), vb, preferred_element_type=F32)
        return m_new, acc

    def block(j, carries, diag):
        start = pl.multiple_of(j * tq, tq)
        return tuple(head_block(hh, start, carries[hh], diag) for hh in range(nh))

    carries = tuple((jnp.full((tq, 1), NEG_BIG, F32), jnp.zeros((tq, 2 * LANES), F32)) for _ in range(nh))
    carries = block(i, carries, True)
    carries = lax.fori_loop(0, i, lambda jj, c: block(i - 1 - jj, c, False), carries)
    for hh in range(nh):
        acc = carries[hh][1]
        o_ref[:, hh * LANES:(hh + 1) * LANES] = (acc[:, :LANES] / acc[:, LANES:]).astype(o_ref.dtype)


def _causal_attention(q, k, v, bsz, seq, heads, scale, tq=512, nh=4):
    tq = min(tq, seq)
    nq = seq // tq
    t = bsz * seq
    wide = nh * LANES
    kv = pl.BlockSpec((seq, wide), lambda b, hh, i: (b, hh), pipeline_mode=pl.Buffered(1))
    qo = pl.BlockSpec((tq, wide), lambda b, hh, i: (b * nq + i, hh))
    return pl.pallas_call(
        functools.partial(_flash_kernel, tq=tq, nh=nh, scale=scale),
        grid=(bsz, heads // nh, nq),
        in_specs=[qo, kv, kv],
        out_specs=qo,
        out_shape=jax.ShapeDtypeStruct((t, heads * LANES), BF16),
        compiler_params=_cparams(("parallel", "parallel", "arbitrary")),
        name="mla_attention",
    )(q, k, v)


def _route(sb, s):
    per = N_EXPERTS // N_GROUPS
    gscore = []
    for g in range(N_GROUPS):
        a, b, c, d = sb[per * g:per * g + per]
        hi1, lo1 = jnp.maximum(a, b), jnp.minimum(a, b)
        hi2, lo2 = jnp.maximum(c, d), jnp.minimum(c, d)
        top1 = jnp.maximum(hi1, hi2)
        top2 = jnp.maximum(jnp.minimum(hi1, hi2), jnp.maximum(lo1, lo2))
        gscore.append(top1 + top2)
    best, gidx = gscore[0], jnp.zeros(gscore[0].shape, I32)
    for g in range(1, N_GROUPS):
        upd = gscore[g] > best
        best = jnp.where(upd, gscore[g], best)
        gidx = jnp.where(upd, g, gidx)

    def pick(rows, j):
        out = rows[j]
        for g in range(1, N_GROUPS):
            out = jnp.where(gidx == g, rows[per * g + j], out)
        return out

    vals = [pick(sb, j) for j in range(per)]
    raws = [pick(s, j) for j in range(per)]
    b1, i1, s1 = vals[0], jnp.zeros(gidx.shape, I32), raws[0]
    for j in range(1, per):
        upd = vals[j] > b1
        b1 = jnp.where(upd, vals[j], b1)
        i1 = jnp.where(upd, j, i1)
        s1 = jnp.where(upd, raws[j], s1)
    b2, i2, s2 = jnp.zeros_like(b1), jnp.full(gidx.shape, -1, I32), jnp.zeros_like(b1)
    for j in range(per):
        upd = (i1 != j) & ((i2 < 0) | (vals[j] > b2))
        b2 = jnp.where(upd, vals[j], b2)
        i2 = jnp.where(upd, j, i2)
        s2 = jnp.where(upd, raws[j], s2)
    denom = s1 + s2
    return per * gidx + i1, per * gidx + i2, s1 / denom, s2 / denom


def _post_kernel(x_ref, mix_ref, mq_ref, kvm_ref, wo_ref, g_ref, b_ref, rw_ref, rb_ref,
                 x1_ref, esel_ref, gate_ref):
    mq = mq_ref[...]
    kvm = kvm_ref[0]
    mem = []
    for h in range(MEM_HEADS):
        hs = slice(h * MEM_HEAD_DIM, (h + 1) * MEM_HEAD_DIM)
        vs = slice(MEM_WIDTH + h * MEM_HEAD_DIM, MEM_WIDTH + (h + 1) * MEM_HEAD_DIM)
        s = lax.dot_general(mq[:, hs], kvm[:, hs], _NT, preferred_element_type=F32) * (MEM_HEAD_DIM ** -0.5)
        p = jnp.exp(s - jnp.max(s, axis=-1, keepdims=True))
        p = p / jnp.sum(p, axis=-1, keepdims=True)
        mem.append(jnp.dot(p.astype(BF16), kvm[:, vs], preferred_element_type=F32))
    y = jnp.dot(mix_ref[...], wo_ref[0:D_MODEL, :], preferred_element_type=F32)
    for h in range(MEM_HEADS):
        r0 = D_MODEL + h * MEM_HEAD_DIM
        y = y + jnp.dot(mem[h].astype(BF16), wo_ref[r0:r0 + MEM_HEAD_DIM, :], preferred_element_type=F32)
    x1 = _layer_norm(DEEPNORM_ALPHA * x_ref[...] + y, g_ref[...], b_ref[...])
    x1_ref[...] = x1
    logits = lax.dot_general(rw_ref[...], x1.astype(BF16), _NT, preferred_element_type=F32)
    s_all = _sigmoid(logits)
    sb_all = s_all + rb_ref[...]
    e1, e2, w1, w2 = _route([sb_all[e:e + 1, :] for e in range(N_EXPERTS)],
                            [s_all[e:e + 1, :] for e in range(N_EXPERTS)])
    esel_ref[0:1, :] = e1
    esel_ref[1:2, :] = e2
    gate_ref[0:1, :] = w1
    gate_ref[1:2, :] = w2


def _post_mixer(x, mix, mq, kvm, w_o, ln_g, ln_b, rw_t, rbias, seq, tm=512):
    t = x.shape[0]
    tm = min(tm, seq)
    per_seq = seq // tm
    row = lambda width: pl.BlockSpec((tm, width), lambda i: (i, 0))
    full = lambda a: pl.BlockSpec(a.shape, lambda i: (0,) * a.ndim)
    g = ln_g.reshape(1, -1)
    b = ln_b.reshape(1, -1)
    return pl.pallas_call(
        _post_kernel,
        grid=(t // tm,),
        in_specs=[row(D_MODEL), row(D_MODEL), row(MEM_WIDTH),
                  pl.BlockSpec((1,) + kvm.shape[1:], lambda i: (i // per_seq, 0, 0)),
                  full(w_o), full(g), full(b), full(rw_t), full(rbias)],
        out_specs=[row(D_MODEL), pl.BlockSpec((TOPK_EXPERTS, tm), lambda i: (0, i)),
                   pl.BlockSpec((TOPK_EXPERTS, tm), lambda i: (0, i))],
        out_shape=[jax.ShapeDtypeStruct((t, D_MODEL), F32), jax.ShapeDtypeStruct((TOPK_EXPERTS, t), I32),
                   jax.ShapeDtypeStruct((TOPK_EXPERTS, t), F32)],
        compiler_params=_cparams(("parallel",)),
        name="post_mixer",
    )(x, mix, mq, kvm, w_o, g, b, rw_t, rbias)


def _moe_rank_kernel(e_ref, rank_ref, cnt_ref, base_ref, *, tr):
    i = pl.program_id(0)

    @pl.when(i == 0)
    def _():
        base_ref[...] = jnp.zeros(base_ref.shape, F32)

    e1 = e_ref[0:1, :]
    e2 = e_ref[1:2, :]
    row = lax.broadcasted_iota(I32, (N_EXPERTS, tr), 0)
    onehot = jnp.where((row == e1) | (row == e2), 1.0, 0.0)
    before = (lax.broadcasted_iota(I32, (tr, tr), 0) < lax.broadcasted_iota(I32, (tr, tr), 1)).astype(BF16)
    seen = jnp.dot(onehot.astype(BF16), before, preferred_element_type=F32) + base_ref[:, 0:1]
    rank_ref[0:1, :] = jnp.sum(jnp.where(row == e1, seen, 0.0), axis=0, keepdims=True).astype(I32)
    rank_ref[1:2, :] = jnp.sum(jnp.where(row == e2, seen, 0.0), axis=0, keepdims=True).astype(I32)
    base_ref[...] = base_ref[...] + jnp.sum(onehot, axis=1, keepdims=True)
    cnt_ref[...] = base_ref[...]


def _moe_rank(esel, tr=512):
    t = esel.shape[1]
    tr = min(tr, t)
    return pl.pallas_call(
        functools.partial(_moe_rank_kernel, tr=tr),
        grid=(t // tr,),
        in_specs=[pl.BlockSpec((TOPK_EXPERTS, tr), lambda i: (0, i))],
        out_specs=[pl.BlockSpec((TOPK_EXPERTS, tr), lambda i: (0, i)),
                   pl.BlockSpec((N_EXPERTS, LANES), lambda i: (0, 0))],
        out_shape=[jax.ShapeDtypeStruct((TOPK_EXPERTS, t), I32), jax.ShapeDtypeStruct((N_EXPERTS, LANES), F32)],
        scratch_shapes=[pltpu.VMEM((N_EXPERTS, LANES), F32)],
        compiler_params=_cparams(("arbitrary",)),
        name="moe_rank",
    )(esel)


def _moe_scatter_kernel(zero_ref, slot_ref, x_ref, buf_ref, zeros_ref, sem, *, ts, n_chunks):
    i = pl.program_id(0)

    def row_copy(tok, k):
        return pltpu.make_async_copy(x_ref.at[pl.ds(tok, 1)], buf_ref.at[pl.ds(slot_ref[0, k, tok], 1)], sem)

    @pl.when(i == 0)
    def _():
        zeros_ref[...] = jnp.zeros(zeros_ref.shape, F32)

        def fill(c, _):
            @pl.when(zero_ref[c] != 0)
            def _():
                cp = pltpu.make_async_copy(
                    zeros_ref, buf_ref.at[pl.ds(pl.multiple_of(c * MOE_CHUNK, MOE_CHUNK), MOE_CHUNK)], sem)
                cp.start()
                cp.wait()
            return 0

        lax.fori_loop(0, n_chunks, fill, 0)

    def start(tok, _):
        for k in range(TOPK_EXPERTS):
            row_copy(tok, k).start()
        return 0

    def wait(tok, _):
        for k in range(TOPK_EXPERTS):
            row_copy(tok, k).wait()
        return 0

    lax.fori_loop(0, ts, start, 0, unroll=8)
    lax.fori_loop(0, ts, wait, 0, unroll=8)


def _moe_scatter(x1, slots, zero_flags, n_chunks, ts=512):
    t = x1.shape[0]
    grid_spec = pltpu.PrefetchScalarGridSpec(
        num_scalar_prefetch=1,
        grid=(t // ts,),
        in_specs=[pl.BlockSpec((1, TOPK_EXPERTS, ts), lambda i, z: (i, 0, 0), memory_space=pltpu.SMEM),
                  pl.BlockSpec((ts, D_MODEL), lambda i, z: (i, 0))],
        out_specs=pl.BlockSpec(memory_space=pl.ANY),
        scratch_shapes=[pltpu.VMEM((MOE_CHUNK, D_MODEL), F32), pltpu.SemaphoreType.DMA(())],
    )
    return pl.pallas_call(
        functools.partial(_moe_scatter_kernel, ts=ts, n_chunks=n_chunks),
        grid_spec=grid_spec,
        out_shape=jax.ShapeDtypeStruct((n_chunks * MOE_CHUNK, D_MODEL), F32),
        compiler_params=_cparams(("arbitrary",)),
        name="moe_scatter",
    )(zero_flags, slots, x1)


def _moe_ffn_kernel(cexp_ref, nused_ref, x_ref, win_ref, wout_ref, y_ref):
    c = pl.program_id(0)

    @pl.when(c < nused_ref[0])
    def _():
        h = jnp.dot(x_ref[...].astype(BF16), win_ref[0], preferred_element_type=F32)
        gate_in = h[:, :EXPERT_FF]
        act = gate_in * _sigmoid(gate_in) * h[:, EXPERT_FF:]
        y_ref[...] = jnp.dot(act.astype(BF16), wout_ref[0], preferred_element_type=F32)

    @pl.when(c >= nused_ref[0])
    def _():
        y_ref[...] = jnp.zeros(y_ref.shape, F32)


def _moe_ffn(buf, chunk_exp, n_used, w_in, w_out):
    n_chunks = chunk_exp.shape[0]
    grid_spec = pltpu.PrefetchScalarGridSpec(
        num_scalar_prefetch=2,
        grid=(n_chunks,),
        in_specs=[pl.BlockSpec((MOE_CHUNK, D_MODEL), lambda c, ce, nu: (c, 0)),
                  pl.BlockSpec((1, D_MODEL, 2 * EXPERT_FF), lambda c, ce, nu: (ce[c], 0, 0)),
                  pl.BlockSpec((1, EXPERT_FF, D_MODEL), lambda c, ce, nu: (ce[c], 0, 0))],
        out_specs=pl.BlockSpec((MOE_CHUNK, D_MODEL), lambda c, ce, nu: (c, 0)),
    )
    return pl.pallas_call(
        _moe_ffn_kernel,
        grid_spec=grid_spec,
        out_shape=jax.ShapeDtypeStruct(buf.shape, F32),
        compiler_params=_cparams(("arbitrary",)),
        name="moe_ffn",
    )(chunk_exp, n_used, buf, w_in, w_out)


def _moe_combine_kernel(slot_ref, x_ref, gate_ref, g_ref, b_ref, yb_ref, o_ref, rows_ref, sem, *, tc):
    def row_copy(tok, k):
        return pltpu.make_async_copy(yb_ref.at[pl.ds(slot_ref[0, k, tok], 1)], rows_ref.at[k, pl.ds(tok, 1)], sem)

    def start(tok, _):
        for k in range(TOPK_EXPERTS):
            row_copy(tok, k).start()
        return 0

    def wait(tok, _):
        for k in range(TOPK_EXPERTS):
            row_copy(tok, k).wait()
        return 0

    lax.fori_loop(0, tc, start, 0, unroll=8)
    lax.fori_loop(0, tc, wait, 0, unroll=8)
    gates = gate_ref[...]
    y = rows_ref[0] * gates[:, 0:1] + rows_ref[1] * gates[:, 1:2]
    o_ref[...] = _layer_norm(DEEPNORM_ALPHA * x_ref[...] + y, g_ref[...], b_ref[...])


def _moe_combine(x1, gates, slots, yb, ln_g, ln_b, tc=512):
    t = x1.shape[0]
    g = ln_g.reshape(1, -1)
    b = ln_b.reshape(1, -1)
    vec = pl.BlockSpec((1, D_MODEL), lambda i: (0, 0))
    return pl.pallas_call(
        functools.partial(_moe_combine_kernel, tc=tc),
        grid=(t // tc,),
        in_specs=[pl.BlockSpec((1, TOPK_EXPERTS, tc), lambda i: (i, 0, 0), memory_space=pltpu.SMEM),
                  pl.BlockSpec((tc, D_MODEL), lambda i: (i, 0)),
                  pl.BlockSpec((tc, LANES), lambda i: (i, 0)),
                  vec, vec,
                  pl.BlockSpec(memory_space=pl.ANY)],
        out_specs=pl.BlockSpec((tc, D_MODEL), lambda i: (i, 0)),
        out_shape=jax.ShapeDtypeStruct((t, D_MODEL), F32),
        scratch_shapes=[pltpu.VMEM((TOPK_EXPERTS, tc, D_MODEL), F32), pltpu.SemaphoreType.DMA(())],
        compiler_params=_cparams(("arbitrary",)),
        name="moe_combine",
    )(slots, x1, gates, g, b, yb)


def _moe(x1, esel, gate_t, w_in, w_out, ln_g, ln_b, tile=512):
    t = x1.shape[0]
    tile = min(tile, t)
    n_chunks = -(-t * TOPK_EXPERTS // MOE_CHUNK) + N_EXPERTS
    rank, cnt = _moe_rank(esel)
    counts = cnt[:, 0].astype(I32)
    padded = (counts + MOE_CHUNK - 1) // MOE_CHUNK * MOE_CHUNK
    pad_end = jnp.cumsum(padded)
    pad_start = pad_end - padded
    expert = jnp.arange(N_EXPERTS, dtype=I32)
    slot = rank + jnp.sum(jnp.where(esel[None] == expert[:, None, None], pad_start[:, None, None], 0), axis=0)
    slots = slot.reshape(TOPK_EXPERTS, t // tile, tile).transpose(1, 0, 2)
    chunk_lo = jnp.arange(n_chunks, dtype=I32) * MOE_CHUNK
    chunk_exp = jnp.minimum(jnp.sum((pad_end[None, :] <= chunk_lo[:, None]).astype(I32), axis=1), N_EXPERTS - 1)
    n_used = (pad_end[-1] // MOE_CHUNK).astype(I32).reshape(1)
    is_last = jnp.any((chunk_lo[:, None] + MOE_CHUNK) == pad_end[None, :], axis=1)
    zero_flags = (is_last | (chunk_lo >= pad_end[-1])).astype(I32)
    buf = _moe_scatter(x1, slots, zero_flags, n_chunks, ts=tile)
    yb = _moe_ffn(buf, chunk_exp, n_used, w_in, w_out)
    gates = _pad_cols(gate_t.T, LANES)
    return _moe_combine(x1, gates, slots, yb, ln_g, ln_b, tc=tile)


def _pad_cols(w, width, offset=0):
    out = jnp.zeros((w.shape[0], width), w.dtype)
    return out.at[:, offset:offset + w.shape[1]].set(w)


def _rope_rows(half, lane_lo, lane_hi):
    inv_freq = ROPE_THETA ** (-jnp.arange(half, dtype=F32) / half)
    lane = jnp.arange(LANES)
    active = (lane >= lane_lo) & (lane < lane_hi)
    pos_in = (lane - lane_lo) % (2 * half)
    freq = jnp.where(active, inv_freq[pos_in % half], 0.0).astype(F32)
    sign = jnp.where(active, jnp.where(pos_in < half, -1.0, 1.0), 0.0).astype(F32)
    return freq.reshape(1, LANES), sign.reshape(1, LANES)


def kernel(x, mem, positions, router_w, router_bias, l0_w_in, l0_mem_wkv, l0_w_o, l0_ln_g, l0_ln_b, l0_moe_w_in, l0_moe_w_out, l1_w_in, l1_conv_w, l1_conv_b, l1_conv_norm_g, l1_conv_norm_b, l1_mem_wkv, l1_w_o, l1_ln_g, l1_ln_b, l1_moe_w_in, l1_moe_w_out, l2_w_in, l2_mem_wkv, l2_w_o, l2_ln_g, l2_ln_b, l2_moe_w_in, l2_moe_w_out, l3_w_in, l3_q_norm_g, l3_kv_norm_g, l3_w_q_up, l3_w_kv_up, l3_mem_wkv, l3_w_o, l3_ln_g, l3_ln_b, l3_moe_w_in, l3_moe_w_out):
    bsz, seq, d = x.shape
    t = bsz * seq
    n_mem = mem.shape[1]
    xt = x.reshape(t, d)
    mem2 = mem.reshape(bsz * n_mem, d)
    pos_b = jnp.broadcast_to(positions.astype(F32).reshape(t, 1), (t, LANES))
    rw_t = router_w.T.astype(BF16)
    rbias = router_bias.astype(F32).reshape(N_EXPERTS, 1)

    def finish_layer(xt, mix, mq, mem_wkv, w_o, ln_g, ln_b, moe_w_in, moe_w_out):
        (kvm,) = _proj(mem2, mem_wkv.astype(BF16), [(0, 2 * MEM_WIDTH, 2 * MEM_WIDTH, 0)], [BF16],
                       tm=n_mem, name="mem_kv")
        kvm = kvm.reshape(bsz, n_mem, 2 * MEM_WIDTH)
        x1, esel, gate_t = _post_mixer(xt, mix, mq, kvm, w_o.astype(BF16), ln_g[0], ln_b[0], rw_t, rbias, seq)
        return _moe(x1, esel, gate_t, moe_w_in.astype(BF16), moe_w_out.astype(BF16), ln_g[1], ln_b[1])

    w0 = l0_w_in.at[:, :SB_HEADS * LANES].multiply(LANES ** -0.5).astype(BF16)
    n0 = w0.shape[1]
    (h0,) = _proj(xt, w0, [(0, n0, n0, 0)], [BF16], name="proj0")
    mix = _sb_attention(h0, bsz, seq)
    mq = h0[:, 3 * D_MODEL:]
    xt = finish_layer(xt, mix, mq, l0_mem_wkv, l0_w_o, l0_ln_g, l0_ln_b, l0_moe_w_in, l0_moe_w_out)

    w1 = l1_w_in.astype(BF16)
    a, g, mq = _proj(xt, w1, [(0, D_MODEL, D_MODEL, 0), (D_MODEL, D_MODEL, D_MODEL, 0),
                              (2 * D_MODEL, MEM_WIDTH, MEM_WIDTH, 0)], [F32, F32, BF16], name="proj1")
    mix = _conformer_conv(a, g, l1_conv_w, l1_conv_b, l1_conv_norm_g, l1_conv_norm_b, seq)
    xt = finish_layer(xt, mix, mq, l1_mem_wkv, l1_w_o, l1_ln_g, l1_ln_b, l1_moe_w_in, l1_moe_w_out)

    w2 = l2_w_in
    c = 0
    parts = []
    for width in (DSA_HEADS * DSA_HEAD_DIM, DSA_HEAD_DIM, DSA_HEAD_DIM, IDX_HEADS * IDX_DIM, IDX_DIM, IDX_HEADS,
                  MEM_WIDTH):
        parts.append(w2[:, c:c + width])
        c += width
    wq, wk, wv, wqi, wki, wwi, wmq = parts
    w2p = jnp.concatenate([wq, _pad_cols(wk, LANES), _pad_cols(wv, LANES), wqi, _pad_cols(wki, LANES),
                           _pad_cols(wwi, LANES), wmq], axis=1).astype(BF16)
    freq, sign = _rope_rows(DSA_HEAD_DIM // 2, 0, LANES)
    cos_a, sin_a = _rope_tables(pos_b, freq, sign)
    rh = DSA_HEAD_DIM // 2
    groups = [(0, 1024, 1024, rh), (1024, LANES, DSA_HEAD_DIM, rh), (1152, LANES, LANES, 0),
              (1280, 512, 512, rh), (1792, LANES, IDX_DIM, rh), (1920, LANES, LANES, 0),
              (2048, MEM_WIDTH, MEM_WIDTH, 0)]
    q, k, v, qi, ki, wi, mq = _proj(xt, w2p, groups, [BF16, BF16, BF16, BF16, BF16, F32, BF16],
                                    rope=(cos_a, sin_a), name="proj2")
    mix = _dsa_attention(q, qi, wi, k, v, ki, bsz, seq)
    xt = finish_layer(xt, mix, mq, l2_mem_wkv, l2_w_o, l2_ln_g, l2_ln_b, l2_moe_w_in, l2_moe_w_out)

    w3 = l3_w_in
    rope_lo = MLA_NOPE
    w_cq = w3[:, :MLA_Q_RANK]
    w_ckv = w3[:, MLA_Q_RANK:MLA_Q_RANK + MLA_KV_RANK]
    w_kr = w3[:, MLA_Q_RANK + MLA_KV_RANK:MLA_Q_RANK + MLA_KV_RANK + MLA_ROPE]
    w_mq = w3[:, MLA_Q_RANK + MLA_KV_RANK + MLA_ROPE:]
    w3p = jnp.concatenate([w_cq, w_ckv, _pad_cols(w_kr, LANES, rope_lo), w_mq], axis=1).astype(BF16)
    freq, sign = _rope_rows(MLA_ROPE // 2, rope_lo, rope_lo + MLA_ROPE)
    cos_b, sin_b = _rope_tables(pos_b, freq, sign)
    groups = [(0, MLA_Q_RANK, MLA_Q_RANK, 0), (MLA_Q_RANK, MLA_KV_RANK, MLA_KV_RANK, 0),
              (MLA_Q_RANK + MLA_KV_RANK, LANES, LANES, MLA_ROPE // 2),
              (MLA_Q_RANK + MLA_KV_RANK + LANES, MEM_WIDTH, MEM_WIDTH, 0)]
    c_q, c_kv, kr, mq = _proj(xt, w3p, groups, [F32, F32, F32, BF16], rope=(cos_b, sin_b), name="proj3")
    qk_dim = MLA_NOPE + MLA_ROPE
    wq_heads = l3_w_q_up.reshape(MLA_Q_RANK, MLA_HEADS, qk_dim)
    wq_p = jnp.zeros((MLA_Q_RANK, MLA_HEADS, LANES), F32).at[:, :, :qk_dim].set(wq_heads)
    wkv_heads = l3_w_kv_up.reshape(MLA_KV_RANK, MLA_HEADS, MLA_NOPE + MLA_V)
    wk_p = jnp.zeros((MLA_KV_RANK, MLA_HEADS, LANES), F32).at[:, :, :MLA_NOPE].set(wkv_heads[:, :, :MLA_NOPE])
    wv_p = wkv_heads[:, :, MLA_NOPE:]
    flat = lambda w: w.reshape(w.shape[0], MLA_HEADS * LANES).astype(BF16)
    qh, kh, vh = _mla_prep(c_q, c_kv, kr, cos_b, sin_b, l3_q_norm_g, l3_kv_norm_g, flat(wq_p), flat(wk_p),
                           flat(wv_p))
    mix = _causal_attention(qh, kh, vh, bsz, seq, MLA_HEADS, qk_dim ** -0.5)
    xt = finish_layer(xt, mix, mq, l3_mem_wkv, l3_w_o, l3_ln_g, l3_ln_b, l3_moe_w_in, l3_moe_w_out)
    return xt.reshape(bsz, seq, d)
```

```python
import functools

import jax
import jax.numpy as jnp
from jax import lax
from jax.experimental import pallas as pl
from jax.experimental.pallas import tpu as pltpu

F32 = jnp.float32
BF16 = jnp.bfloat16
I32 = jnp.int32

D_MODEL = 1024
DEPTH = 4
ROPE_THETA = 10000.0
LN_EPS = 1e-5
RMS_EPS = 1e-6
DEEPNORM_ALPHA = (2.0 * DEPTH) ** 0.25
SB_HEADS = 8
CONV_WIDTH = 31
DSA_HEADS = 16
DSA_HEAD_DIM = 64
IDX_HEADS = 8
IDX_DIM = 64
IDX_TOPK_MAX = 256
MLA_HEADS = 8
MLA_Q_RANK = 384
MLA_KV_RANK = 256
MLA_NOPE = 64
MLA_ROPE = 32
MLA_V = 128
MEM_HEADS = 4
MEM_HEAD_DIM = 64
MEM_WIDTH = 256
N_EXPERTS = 16
N_GROUPS = 4
EXPERT_FF = 512
TOPK_EXPERTS = 2
MOE_CHUNK = 512

LANES = 128
VMEM_LIMIT_BYTES = 56 * 1024 * 1024
NEG_BIG = -1e30
MAX_SAFE_SHIFT = 40.0
INT_MIN = -2 ** 31

_NT = (((1,), (1,)), ((), ()))


def _cparams(sem):
    return pltpu.CompilerParams(dimension_semantics=sem, vmem_limit_bytes=VMEM_LIMIT_BYTES)


def _layer_norm(v, g, b):
    mu = jnp.mean(v, axis=-1, keepdims=True)
    d = v - mu
    var = jnp.mean(d * d, axis=-1, keepdims=True)
    return d * lax.rsqrt(var + LN_EPS) * g + b


def _sigmoid(v):
    return 1.0 / (1.0 + jnp.exp(-v))


def _rope_table_kernel(pos_ref, freq_ref, sign_ref, cos_ref, sin_ref):
    ang = pos_ref[...] * freq_ref[...]
    cos_ref[...] = jnp.cos(ang)
    sin_ref[...] = jnp.sin(ang) * sign_ref[...]


def _rope_tables(pos_b, freq_row, sign_row, tm=512):
    t = pos_b.shape[0]
    row = pl.BlockSpec((tm, LANES), lambda i: (i, 0))
    one = pl.BlockSpec((1, LANES), lambda i: (0, 0))
    return pl.pallas_call(
        _rope_table_kernel,
        grid=(t // tm,),
        in_specs=[row, one, one],
        out_specs=[row, row],
        out_shape=[jax.ShapeDtypeStruct((t, LANES), F32)] * 2,
        compiler_params=_cparams(("parallel",)),
        name="rope_tables",
    )(pos_b, freq_row, sign_row)


def _rope128(v, cos, sin, half):
    lane = lax.broadcasted_iota(I32, v.shape, 1)
    first = (lane & (2 * half - 1)) < half
    fwd = pltpu.roll(v, LANES - half, 1)
    bwd = pltpu.roll(v, half, 1)
    return v * cos + jnp.where(first, fwd, bwd) * sin


def _proj_kernel(*refs, groups, has_rope):
    x_ref, w_ref = refs[0], refs[1]
    nin = 4 if has_rope else 2
    outs = refs[nin:]
    x = x_ref[...].astype(BF16)
    for (start, width, out_width, rope_half), o_ref in zip(groups, outs):
        acc = jnp.dot(x, w_ref[:, start:start + width], preferred_element_type=F32)
        if rope_half:
            cos = refs[2][...]
            sin = refs[3][...]
            for c in range(width // LANES):
                r = _rope128(acc[:, c * LANES:(c + 1) * LANES], cos, sin, rope_half)
                lo = c * LANES
                hi = min(lo + LANES, out_width)
                o_ref[:, lo:hi] = r[:, :hi - lo].astype(o_ref.dtype)
        else:
            o_ref[...] = acc[:, :out_width].astype(o_ref.dtype)


def _proj(x, w, groups, out_dtypes, rope=None, tm=512, name="proj"):
    m, k = x.shape
    n = w.shape[1]
    tm = min(tm, m)
    in_specs = [pl.BlockSpec((tm, k), lambda i: (i, 0)), pl.BlockSpec((k, n), lambda i: (0, 0))]
    args = [x, w]
    if rope is not None:
        in_specs += [pl.BlockSpec((tm, LANES), lambda i: (i, 0))] * 2
        args += list(rope)
    out_specs = [pl.BlockSpec((tm, g[2]), lambda i: (i, 0)) for g in groups]
    out_shape = [jax.ShapeDtypeStruct((m, g[2]), dt) for g, dt in zip(groups, out_dtypes)]
    return pl.pallas_call(
        functools.partial(_proj_kernel, groups=tuple(groups), has_rope=rope is not None),
        grid=(m // tm,),
        in_specs=in_specs,
        out_specs=out_specs,
        out_shape=out_shape,
        compiler_params=_cparams(("parallel",)),
        name=name,
    )(*args)


def _sb_kernel(q_ref, k_ref, v_ref, o_ref, *, tq, ck, nh):
    i = pl.program_id(2)
    r = lax.broadcasted_iota(I32, (ck, ck), 0)
    c = lax.broadcasted_iota(I32, (ck, ck), 1)
    suffix = (r >= c).astype(BF16)
    row = lax.broadcasted_iota(I32, (tq, ck), 0)
    col = lax.broadcasted_iota(I32, (tq, ck), 1)

    def head_block(hh, start, carry, diag):
        tail, acc = carry
        hs = slice(hh * LANES, (hh + 1) * LANES)
        kb = k_ref[pl.ds(start, tq), hs]
        z = lax.dot_general(q_ref[:, hs], kb, _NT, preferred_element_type=F32)
        for cc in reversed(range(tq // ck)):
            zc = z[:, cc * ck:(cc + 1) * ck]
            sp = jnp.maximum(zc, 0.0) + jnp.log(1.0 + jnp.exp(-jnp.abs(zc)))
            if diag:
                strict = col + cc * ck < row
                sp = jnp.where(strict, sp, 0.0)
            cum = jnp.dot(sp.astype(BF16), suffix, preferred_element_type=F32)
            a = jnp.exp(zc - cum - tail)
            if diag:
                a = jnp.where(strict, a, 0.0)
            vb = v_ref[pl.ds(pl.multiple_of(start + cc * ck, ck), ck), hs]
            acc = acc + jnp.dot(a.astype(BF16), vb, preferred_element_type=F32)
            tail = tail + jnp.sum(sp, axis=-1, keepdims=True)
        return tail, acc

    def block(j, carries, diag):
        start = pl.multiple_of(j * tq, tq)
        return tuple(head_block(hh, start, carries[hh], diag) for hh in range(nh))

    carries = tuple((jnp.zeros((tq, 1), F32), jnp.zeros((tq, LANES), F32)) for _ in range(nh))
    carries = block(i, carries, True)
    carries = lax.fori_loop(0, i, lambda jj, c: block(i - 1 - jj, c, False), carries)
    for hh in range(nh):
        o_ref[:, hh * LANES:(hh + 1) * LANES] = carries[hh][1].astype(o_ref.dtype)


def _sb_attention(h, bsz, seq, tq=512, ck=256, nh=2):
    tq = min(tq, seq)
    ck = min(ck, tq)
    nq = seq // tq
    t = bsz * seq
    hg = SB_HEADS // nh
    wide = nh * LANES
    return pl.pallas_call(
        functools.partial(_sb_kernel, tq=tq, ck=ck, nh=nh),
        grid=(bsz, hg, nq),
        in_specs=[
            pl.BlockSpec((tq, wide), lambda b, hh, i: (b * nq + i, hh)),
            pl.BlockSpec((seq, wide), lambda b, hh, i: (b, hg + hh)),
            pl.BlockSpec((seq, wide), lambda b, hh, i: (b, 2 * hg + hh)),
        ],
        out_specs=pl.BlockSpec((tq, wide), lambda b, hh, i: (b * nq + i, hh)),
        out_shape=jax.ShapeDtypeStruct((t, SB_HEADS * LANES), BF16),
        compiler_params=_cparams(("parallel", "parallel", "arbitrary")),
        name="sb_attention",
    )(h, h, h)


def _conv_kernel(a_ref, g_ref, ap_ref, gp_ref, w_ref, cb_ref, ng_ref, nb_ref, o_ref, hext_ref, y_ref,
                 *, tm, halo, tiles_per_seq):
    i = pl.program_id(0)
    hext_ref[halo:halo + tm, :] = a_ref[...] * _sigmoid(g_ref[...])
    prev = ap_ref[...] * _sigmoid(gp_ref[...])
    first = (i % tiles_per_seq) == 0
    hext_ref[0:halo, :] = jnp.where(first, 0.0, prev)
    off = halo - (CONV_WIDTH - 1)
    for c in range(D_MODEL // LANES):
        cs = slice(c * LANES, (c + 1) * LANES)
        acc = jnp.zeros((tm, LANES), F32)
        for j in range(CONV_WIDTH):
            acc = acc + w_ref[j:j + 1, cs] * hext_ref[off + j:off + j + tm, cs]
        y_ref[:, cs] = acc
    y = y_ref[...] + cb_ref[...]
    y = _layer_norm(y, ng_ref[...], nb_ref[...])
    o_ref[...] = (y * _sigmoid(y)).astype(o_ref.dtype)


def _conformer_conv(a, g, conv_w, conv_b, norm_g, norm_b, seq, tm=256, halo=32):
    t = a.shape[0]
    tm = min(tm, seq)
    r = tm // halo
    cur = pl.BlockSpec((tm, D_MODEL), lambda i: (i, 0))
    prv = pl.BlockSpec((halo, D_MODEL), lambda i: (jnp.maximum(i * r - 1, 0), 0))
    vec = pl.BlockSpec((1, D_MODEL), lambda i: (0, 0))
    return pl.pallas_call(
        functools.partial(_conv_kernel, tm=tm, halo=halo, tiles_per_seq=seq // tm),
        grid=(t // tm,),
        in_specs=[cur, cur, prv, prv, pl.BlockSpec((CONV_WIDTH, D_MODEL), lambda i: (0, 0)), vec, vec, vec],
        out_specs=cur,
        out_shape=jax.ShapeDtypeStruct((t, D_MODEL), BF16),
        scratch_shapes=[pltpu.VMEM((tm + halo, D_MODEL), F32), pltpu.VMEM((tm, D_MODEL), F32)],
        compiler_params=_cparams(("parallel",)),
        name="conformer_conv",
    )(a, g, a, g, conv_w, conv_b.reshape(1, -1), norm_g.reshape(1, -1), norm_b.reshape(1, -1))


def _dsa_kernel(q_ref, qi_ref, wi_ref, k_ref, v_ref, ki_ref, o_ref,
                qs_ref, qis_ref, keys_ref, eq_hi_ref, mrun_ref, mb_ref, acc_ref, kmax2_ref,
                *, tq, tkb, topk, seq, idx_bits, scale):
    i = pl.program_id(1)
    nkb = (i * tq + tq + tkb - 1) // tkb

    @pl.when(i == 0)
    def _():
        rows = min(tkb, seq)

        def body(c, m):
            kk = k_ref[pl.ds(pl.multiple_of(c * rows, rows), rows), :].astype(F32)
            return jnp.maximum(m, jnp.sum(kk * kk, axis=-1, keepdims=True))

        kmax2_ref[0] = jnp.max(lax.fori_loop(0, seq // rows, body, jnp.zeros((rows, 1), F32)))

    for h in range(DSA_HEADS):
        qs_ref[h * tq:(h + 1) * tq, :] = q_ref[:, h * DSA_HEAD_DIM:(h + 1) * DSA_HEAD_DIM]
    for h in range(IDX_HEADS):
        qis_ref[h * tq:(h + 1) * tq, :] = qi_ref[:, h * IDX_DIM:(h + 1) * IDX_DIM]
    wi = wi_ref[...] * (IDX_HEADS ** -0.5 * IDX_DIM ** -0.5)
    wcols = [wi[:, h:h + 1] for h in range(IDX_HEADS)]
    q_pos = i * tq + lax.broadcasted_iota(I32, (tq, 1), 0)
    col0 = lax.broadcasted_iota(I32, (tq, tkb), 1)

    def scores(jb, _):
        start = pl.multiple_of(jb * tkb, tkb)
        kib = ki_ref[pl.ds(start, tkb), :]
        dots = lax.dot_general(qis_ref[...], kib, _NT, preferred_element_type=F32)
        d3 = dots.reshape(IDX_HEADS, tq, tkb)
        sc = wcols[0] * jnp.maximum(d3[0], 0.0)
        for h in range(1, IDX_HEADS):
            sc = sc + wcols[h] * jnp.maximum(d3[h], 0.0)
        sc = sc + 0.0
        bits = lax.bitcast_convert_type(sc, I32)
        key = bits ^ ((bits >> 31) & 0x7FFFFFFF)
        key = jnp.where(col0 + start <= q_pos, key, INT_MIN)
        keys_ref[:, pl.ds(start, tkb)] = key
        return 0

    lax.fori_loop(0, nkb, scores, 0)

    def count_if(pred):
        def body(c, acc):
            start = pl.multiple_of(c * tkb, tkb)
            ind = jnp.where(pred(keys_ref[:, pl.ds(start, tkb)], start), 1.0, 0.0)
            for s in range(tkb // LANES):
                acc = acc + ind[:, s * LANES:(s + 1) * LANES]
            return acc
        acc = lax.fori_loop(0, nkb, body, jnp.zeros((tq, LANES), F32))
        return jnp.sum(acc, axis=-1, keepdims=True)

    kf = float(topk)
    short = count_if(lambda kb, st: kb > INT_MIN) < kf
    n_pos = count_if(lambda kb, st: kb >= 0)
    tau = jnp.where(n_pos >= kf, 0, INT_MIN).astype(I32)
    n_ge = jnp.where(n_pos >= kf, n_pos, 2.0 * seq)

    def unsettled(n_ge):
        return jnp.max(jnp.where(short | (n_ge == kf), 0.0, 1.0))

    def bisect(state):
        it, tau, n_ge, _ = state
        cand = tau + jnp.left_shift(jnp.int32(1), 30 - it)
        n_cand = count_if(lambda kb, st: kb >= cand)
        keep = n_cand >= kf
        tau = jnp.where(keep, cand, tau)
        n_ge = jnp.where(keep, n_cand, n_ge)
        return it + 1, tau, n_ge, unsettled(n_ge)

    _, tau, n_ge, _ = lax.while_loop(lambda st: (st[0] < 31) & (st[3] > 0.0), bisect,
                                     (jnp.int32(0), tau, n_ge, unsettled(n_ge)))
    n_gt = count_if(lambda kb, st: kb > tau)
    need = kf - n_gt
    eq_hi_ref[...] = jnp.where(short, -1, seq).astype(I32)
    tied = jnp.max(jnp.where((n_ge > kf) & jnp.logical_not(short), 1.0, 0.0)) > 0.0

    @pl.when(tied)
    def _():
        def step(it, p):
            cand = p + jnp.left_shift(jnp.int32(1), idx_bits - it)
            cnt = count_if(lambda kb, st: (kb == tau) & (col0 + st < cand))
            return jnp.where((cand <= seq) & (cnt < need), cand, p)

        p = lax.fori_loop(0, idx_bits + 1, step, jnp.zeros((tq, 1), I32))
        eq_hi_ref[...] = jnp.where(short, -1, p)

    eq_hi = eq_hi_ref[...]

    def selected(start):
        kb = keys_ref[:, pl.ds(start, tkb)]
        return (kb > tau) | ((kb == tau) & (col0 + start <= eq_hi))

    qf = qs_ref[...].astype(F32)
    q_norm2 = jnp.sum(qf * qf, axis=-1, keepdims=True).reshape(DSA_HEADS, tq, 1)
    bound = jnp.sqrt(q_norm2 * kmax2_ref[0]) * scale
    bound_ok = jnp.max(bound) < MAX_SAFE_SHIFT

    @pl.when(bound_ok)
    def _():
        mb_ref[...] = jnp.broadcast_to(bound, mb_ref.shape)

    @pl.when(jnp.logical_not(bound_ok))
    def _():
        mrun_ref[...] = jnp.full(mrun_ref.shape, NEG_BIG, F32)

        def row_max(jb, _):
            start = pl.multiple_of(jb * tkb, tkb)
            s = lax.dot_general(qs_ref[...], k_ref[pl.ds(start, tkb), :], _NT, preferred_element_type=F32)
            s3 = jnp.where(selected(start)[None], s.reshape(DSA_HEADS, tq, tkb), NEG_BIG)
            m = mrun_ref[...]
            for c in range(tkb // LANES):
                m = jnp.maximum(m, s3[:, :, c * LANES:(c + 1) * LANES])
            mrun_ref[...] = m
            return 0

        lax.fori_loop(0, nkb, row_max, 0)
        m_row = jnp.max(mrun_ref[...], axis=-1, keepdims=True) * scale
        mb_ref[...] = jnp.broadcast_to(m_row, mb_ref.shape)

    acc_ref[...] = jnp.zeros(acc_ref.shape, F32)
    vlane = lax.broadcasted_iota(I32, (tkb, LANES), 1)

    def attend(jb, _):
        start = pl.multiple_of(jb * tkb, tkb)
        kb = k_ref[pl.ds(start, tkb), :]
        vb = jnp.where(vlane < DSA_HEAD_DIM, v_ref[pl.ds(start, tkb), :], 1.0).astype(BF16)
        s = lax.dot_general(qs_ref[...], kb, _NT, preferred_element_type=F32)
        t3 = s.reshape(DSA_HEADS, tq, tkb) * scale - mb_ref[...]
        p = jnp.where(selected(start)[None], jnp.exp(t3), 0.0)
        acc_ref[...] += jnp.dot(p.reshape(DSA_HEADS * tq, tkb).astype(BF16), vb, preferred_element_type=F32)
        return 0

    lax.fori_loop(0, nkb, attend, 0)
    acc = acc_ref[...]
    out = acc / pltpu.roll(acc, LANES - DSA_HEAD_DIM, 1)
    for h in range(DSA_HEADS):
        o_ref[:, h * DSA_HEAD_DIM:(h + 1) * DSA_HEAD_DIM] = (
            out[h * tq:(h + 1) * tq, :DSA_HEAD_DIM].astype(o_ref.dtype))


def _dsa_attention(q, qi, wi, k, v, ki, bsz, seq, tq=128, tkb=512):
    t = bsz * seq
    tq = min(tq, seq)
    tkb = min(tkb, seq)
    nq = seq // tq
    topk = min(IDX_TOPK_MAX, seq // 4)
    idx_bits = max(seq - 1, 1).bit_length()
    qrow = lambda width: pl.BlockSpec((tq, width), lambda b, i: (b * nq + i, 0))
    kv = lambda width: pl.BlockSpec((seq, width), lambda b, i: (b, 0), pipeline_mode=pl.Buffered(1))
    return pl.pallas_call(
        functools.partial(_dsa_kernel, tq=tq, tkb=tkb, topk=topk, seq=seq, idx_bits=idx_bits,
                          scale=DSA_HEAD_DIM ** -0.5),
        grid=(bsz, nq),
        in_specs=[qrow(DSA_HEADS * DSA_HEAD_DIM), qrow(IDX_HEADS * IDX_DIM), qrow(LANES),
                  kv(DSA_HEAD_DIM), kv(LANES), kv(IDX_DIM)],
        out_specs=qrow(DSA_HEADS * DSA_HEAD_DIM),
        out_shape=jax.ShapeDtypeStruct((t, DSA_HEADS * DSA_HEAD_DIM), BF16),
        scratch_shapes=[
            pltpu.VMEM((DSA_HEADS * tq, DSA_HEAD_DIM), BF16),
            pltpu.VMEM((IDX_HEADS * tq, IDX_DIM), BF16),
            pltpu.VMEM((tq, seq), I32),
            pltpu.VMEM((tq, 1), I32),
            pltpu.VMEM((DSA_HEADS, tq, LANES), F32),
            pltpu.VMEM((DSA_HEADS, tq, tkb), F32),
            pltpu.VMEM((DSA_HEADS * tq, LANES), F32),
            pltpu.SMEM((1,), F32),
        ],
        compiler_params=_cparams(("parallel", "arbitrary")),
        name="dsa_attention",
    )(q, qi, wi, k, v, ki)


def _mla_prep_kernel(cq_ref, ckv_ref, kr_ref, cos_ref, sin_ref, qg_ref, kvg_ref, wq_ref, wk_ref, wv_ref,
                     q_ref, k_ref, v_ref):
    def rms(c, g):
        return c * lax.rsqrt(jnp.mean(c * c, axis=-1, keepdims=True) + RMS_EPS) * g

    cos = cos_ref[...]
    sin = sin_ref[...]
    qn = rms(cq_ref[...], qg_ref[...]).astype(BF16)
    kvn = rms(ckv_ref[...], kvg_ref[...]).astype(BF16)
    kr = kr_ref[...]
    for h in range(MLA_HEADS):
        hs = slice(h * LANES, (h + 1) * LANES)
        qh = jnp.dot(qn, wq_ref[:, hs], preferred_element_type=F32)
        q_ref[:, hs] = _rope128(qh, cos, sin, MLA_ROPE // 2).astype(q_ref.dtype)
        kh = jnp.dot(kvn, wk_ref[:, hs], preferred_element_type=F32)
        k_ref[:, hs] = (kh + kr).astype(k_ref.dtype)
        v_ref[:, hs] = jnp.dot(kvn, wv_ref[:, hs], preferred_element_type=F32).astype(v_ref.dtype)


def _mla_prep(c_q, c_kv, kr, cos, sin, q_norm_g, kv_norm_g, wq, wk, wv, tm=512):
    t = c_q.shape[0]
    tm = min(tm, t)
    row = lambda width: pl.BlockSpec((tm, width), lambda i: (i, 0))
    full = lambda a: pl.BlockSpec(a.shape, lambda i: (0, 0))
    qg = q_norm_g.reshape(1, -1)
    kvg = kv_norm_g.reshape(1, -1)
    width = MLA_HEADS * LANES
    return pl.pallas_call(
        _mla_prep_kernel,
        grid=(t // tm,),
        in_specs=[row(MLA_Q_RANK), row(MLA_KV_RANK), row(LANES), row(LANES), row(LANES),
                  full(qg), full(kvg), full(wq), full(wk), full(wv)],
        out_specs=[row(width)] * 3,
        out_shape=[jax.ShapeDtypeStruct((t, width), BF16)] * 3,
        compiler_params=_cparams(("parallel",)),
        name="mla_prep",
    )(c_q, c_kv, kr, cos, sin, qg, kvg, wq, wk, wv)


def _flash_kernel(q_ref, k_ref, v_ref, o_ref, *, tq, nh):
    i = pl.program_id(2)
    row = lax.broadcasted_iota(I32, (tq, tq), 0)
    col = lax.broadcasted_iota(I32, (tq, tq), 1)
    causal = col <= row
    ones = jnp.ones((tq, LANES), BF16)

    def head_block(hh, start, carry, diag):
        m, acc = carry
        hs = slice(hh * LANES, (hh + 1) * LANES)
        kb = k_ref[pl.ds(start, tq), hs]
        vb = jnp.concatenate([v_ref[pl.ds(start, tq), hs], ones], axis=1)
        s = lax.dot_general(q_ref[:, hs], kb, _NT, preferred_element_type=F32)
        if diag:
            s = jnp.where(causal, s, NEG_BIG)
        m_new = jnp.maximum(m, jnp.max(s, axis=-1, keepdims=True))
        alpha = jnp.exp(m - m_new)
        p = jnp.exp(s - m_new)
        acc = alpha * acc + jnp.dot(p.astype(BF16), vb, preferred_element_type=F32)
        return m_new, acc

    def block(j, carries, diag):
        start = pl.multiple_of(j * tq, tq)
        return tuple(head_block(hh, start, carries[hh], diag) for hh in range(nh))

    carries = tuple((jnp.full((tq, 1), NEG_BIG, F32), jnp.zeros((tq, 2 * LANES), F32)) for _ in range(nh))
    carries = block(i, carries, True)
    carries = lax.fori_loop(0, i, lambda jj, c: block(i - 1 - jj, c, False), carries)
    for hh in range(nh):
        acc = carries[hh][1]
        o_ref[:, hh * LANES:(hh + 1) * LANES] = (acc[:, :LANES] / acc[:, LANES:]).astype(o_ref.dtype)


def _causal_attention(q, k, v, bsz, seq, heads, tq=512, nh=4):
    tq = min(tq, seq)
    nq = seq // tq
    t = bsz * seq
    wide = nh * LANES
    kv = pl.BlockSpec((seq, wide), lambda b, hh, i: (b, hh), pipeline_mode=pl.Buffered(1))
    qo = pl.BlockSpec((tq, wide), lambda b, hh, i: (b * nq + i, hh))
    return pl.pallas_call(
        functools.partial(_flash_kernel, tq=tq, nh=nh),
        grid=(bsz, heads // nh, nq),
        in_specs=[qo, kv, kv],
        out_specs=qo,
        out_shape=jax.ShapeDtypeStruct((t, heads * LANES), BF16),
        compiler_params=_cparams(("parallel", "parallel", "arbitrary")),
        name="mla_attention",
    )(q, k, v)


def _route(sb, s):
    per = N_EXPERTS // N_GROUPS
    gscore = []
    for g in range(N_GROUPS):
        a, b, c, d = sb[per * g:per * g + per]
        hi1, lo1 = jnp.maximum(a, b), jnp.minimum(a, b)
        hi2, lo2 = jnp.maximum(c, d), jnp.minimum(c, d)
        top1 = jnp.maximum(hi1, hi2)
        top2 = jnp.maximum(jnp.minimum(hi1, hi2), jnp.maximum(lo1, lo2))
        gscore.append(top1 + top2)
    best, gidx = gscore[0], jnp.zeros(gscore[0].shape, I32)
    for g in range(1, N_GROUPS):
        upd = gscore[g] > best
        best = jnp.where(upd, gscore[g], best)
        gidx = jnp.where(upd, g, gidx)

    def pick(rows, j):
        out = rows[j]
        for g in range(1, N_GROUPS):
            out = jnp.where(gidx == g, rows[per * g + j], out)
        return out

    vals = [pick(sb, j) for j in range(per)]
    raws = [pick(s, j) for j in range(per)]
    b1, i1, s1 = vals[0], jnp.zeros(gidx.shape, I32), raws[0]
    for j in range(1, per):
        upd = vals[j] > b1
        b1 = jnp.where(upd, vals[j], b1)
        i1 = jnp.where(upd, j, i1)
        s1 = jnp.where(upd, raws[j], s1)
    b2, i2, s2 = jnp.zeros_like(b1), jnp.full(gidx.shape, -1, I32), jnp.zeros_like(b1)
    for j in range(per):
        upd = (i1 != j) & ((i2 < 0) | (vals[j] > b2))
        b2 = jnp.where(upd, vals[j], b2)
        i2 = jnp.where(upd, j, i2)
        s2 = jnp.where(upd, raws[j], s2)
    denom = s1 + s2
    return per * gidx + i1, per * gidx + i2, s1 / denom, s2 / denom


def _post_kernel(x_ref, mix_ref, mq_ref, kvm_ref, wo_ref, g_ref, b_ref, rw_ref, rb_ref,
                 x1_ref, esel_ref, gate_ref):
    mq = mq_ref[...]
    kvm = kvm_ref[0]
    mem = []
    for h in range(MEM_HEADS):
        hs = slice(h * MEM_HEAD_DIM, (h + 1) * MEM_HEAD_DIM)
        vs = slice(MEM_WIDTH + h * MEM_HEAD_DIM, MEM_WIDTH + (h + 1) * MEM_HEAD_DIM)
        s = lax.dot_general(mq[:, hs], kvm[:, hs], _NT, preferred_element_type=F32) * (MEM_HEAD_DIM ** -0.5)
        p = jnp.exp(s - jnp.max(s, axis=-1, keepdims=True))
        p = p / jnp.sum(p, axis=-1, keepdims=True)
        mem.append(jnp.dot(p.astype(BF16), kvm[:, vs], preferred_element_type=F32))
    y = jnp.dot(mix_ref[...], wo_ref[0:D_MODEL, :], preferred_element_type=F32)
    for h in range(MEM_HEADS):
        r0 = D_MODEL + h * MEM_HEAD_DIM
        y = y + jnp.dot(mem[h].astype(BF16), wo_ref[r0:r0 + MEM_HEAD_DIM, :], preferred_element_type=F32)
    x1 = _layer_norm(DEEPNORM_ALPHA * x_ref[...] + y, g_ref[...], b_ref[...])
    x1_ref[...] = x1
    logits = lax.dot_general(rw_ref[...], x1.astype(BF16), _NT, preferred_element_type=F32)
    s_all = _sigmoid(logits)
    sb_all = s_all + rb_ref[...]
    e1, e2, w1, w2 = _route([sb_all[e:e + 1, :] for e in range(N_EXPERTS)],
                            [s_all[e:e + 1, :] for e in range(N_EXPERTS)])
    esel_ref[0:1, :] = e1
    esel_ref[1:2, :] = e2
    gate_ref[0:1, :] = w1
    gate_ref[1:2, :] = w2


def _post_mixer(x, mix, mq, kvm, w_o, ln_g, ln_b, rw_t, rbias, seq, tm=512):
    t = x.shape[0]
    tm = min(tm, seq)
    per_seq = seq // tm
    row = lambda width: pl.BlockSpec((tm, width), lambda i: (i, 0))
    full = lambda a: pl.BlockSpec(a.shape, lambda i: (0,) * a.ndim)
    g = ln_g.reshape(1, -1)
    b = ln_b.reshape(1, -1)
    return pl.pallas_call(
        _post_kernel,
        grid=(t // tm,),
        in_specs=[row(D_MODEL), row(D_MODEL), row(MEM_WIDTH),
                  pl.BlockSpec((1,) + kvm.shape[1:], lambda i: (i // per_seq, 0, 0)),
                  full(w_o), full(g), full(b), full(rw_t), full(rbias)],
        out_specs=[row(D_MODEL), pl.BlockSpec((TOPK_EXPERTS, tm), lambda i: (0, i)),
                   pl.BlockSpec((TOPK_EXPERTS, tm), lambda i: (0, i))],
        out_shape=[jax.ShapeDtypeStruct((t, D_MODEL), F32), jax.ShapeDtypeStruct((TOPK_EXPERTS, t), I32),
                   jax.ShapeDtypeStruct((TOPK_EXPERTS, t), F32)],
        compiler_params=_cparams(("parallel",)),
        name="post_mixer",
    )(x, mix, mq, kvm, w_o, g, b, rw_t, rbias)


def _moe_rank_kernel(e_ref, rank_ref, cnt_ref, base_ref, *, tr):
    i = pl.program_id(0)

    @pl.when(i == 0)
    def _():
        base_ref[...] = jnp.zeros(base_ref.shape, F32)

    e1 = e_ref[0:1, :]
    e2 = e_ref[1:2, :]
    row = lax.broadcasted_iota(I32, (N_EXPERTS, tr), 0)
    onehot = jnp.where((row == e1) | (row == e2), 1.0, 0.0)
    before = (lax.broadcasted_iota(I32, (tr, tr), 0) < lax.broadcasted_iota(I32, (tr, tr), 1)).astype(BF16)
    seen = jnp.dot(onehot.astype(BF16), before, preferred_element_type=F32) + base_ref[:, 0:1]
    rank_ref[0:1, :] = jnp.sum(jnp.where(row == e1, seen, 0.0), axis=0, keepdims=True).astype(I32)
    rank_ref[1:2, :] = jnp.sum(jnp.where(row == e2, seen, 0.0), axis=0, keepdims=True).astype(I32)
    base_ref[...] = base_ref[...] + jnp.sum(onehot, axis=1, keepdims=True)
    cnt_ref[...] = base_ref[...]


def _moe_rank(esel, tr=512):
    t = esel.shape[1]
    tr = min(tr, t)
    return pl.pallas_call(
        functools.partial(_moe_rank_kernel, tr=tr),
        grid=(t // tr,),
        in_specs=[pl.BlockSpec((TOPK_EXPERTS, tr), lambda i: (0, i))],
        out_specs=[pl.BlockSpec((TOPK_EXPERTS, tr), lambda i: (0, i)),
                   pl.BlockSpec((N_EXPERTS, LANES), lambda i: (0, 0))],
        out_shape=[jax.ShapeDtypeStruct((TOPK_EXPERTS, t), I32), jax.ShapeDtypeStruct((N_EXPERTS, LANES), F32)],
        scratch_shapes=[pltpu.VMEM((N_EXPERTS, LANES), F32)],
        compiler_params=_cparams(("arbitrary",)),
        name="moe_rank",
    )(esel)


def _moe_scatter_kernel(zero_ref, slot_ref, x_ref, buf_ref, zeros_ref, sem, *, ts, n_chunks):
    i = pl.program_id(0)

    def row_copy(tok, k):
        return pltpu.make_async_copy(x_ref.at[pl.ds(tok, 1)], buf_ref.at[pl.ds(slot_ref[0, k, tok], 1)], sem)

    @pl.when(i == 0)
    def _():
        zeros_ref[...] = jnp.zeros(zeros_ref.shape, F32)

        def fill(c, _):
            @pl.when(zero_ref[c] != 0)
            def _():
                cp = pltpu.make_async_copy(
                    zeros_ref, buf_ref.at[pl.ds(pl.multiple_of(c * MOE_CHUNK, MOE_CHUNK), MOE_CHUNK)], sem)
                cp.start()
                cp.wait()
            return 0

        lax.fori_loop(0, n_chunks, fill, 0)

    def start(tok, _):
        for k in range(TOPK_EXPERTS):
            row_copy(tok, k).start()
        return 0

    def wait(tok, _):
        for k in range(TOPK_EXPERTS):
            row_copy(tok, k).wait()
        return 0

    lax.fori_loop(0, ts, start, 0, unroll=8)
    lax.fori_loop(0, ts, wait, 0, unroll=8)


def _moe_scatter(x1, slots, zero_flags, n_chunks, ts=512):
    t = x1.shape[0]
    grid_spec = pltpu.PrefetchScalarGridSpec(
        num_scalar_prefetch=1,
        grid=(t // ts,),
        in_specs=[pl.BlockSpec((1, TOPK_EXPERTS, ts), lambda i, z: (i, 0, 0), memory_space=pltpu.SMEM),
                  pl.BlockSpec((ts, D_MODEL), lambda i, z: (i, 0))],
        out_specs=pl.BlockSpec(memory_space=pl.ANY),
        scratch_shapes=[pltpu.VMEM((MOE_CHUNK, D_MODEL), F32), pltpu.SemaphoreType.DMA(())],
    )
    return pl.pallas_call(
        functools.partial(_moe_scatter_kernel, ts=ts, n_chunks=n_chunks),
        grid_spec=grid_spec,
        out_shape=jax.ShapeDtypeStruct((n_chunks * MOE_CHUNK, D_MODEL), F32),
        compiler_params=_cparams(("arbitrary",)),
        name="moe_scatter",
    )(zero_flags, slots, x1)


def _moe_ffn_kernel(cexp_ref, nused_ref, x_ref, win_ref, wout_ref, y_ref):
    c = pl.program_id(0)

    @pl.when(c < nused_ref[0])
    def _():
        h = jnp.dot(x_ref[...].astype(BF16), win_ref[0], preferred_element_type=F32)
        gate_in = h[:, :EXPERT_FF]
        act = gate_in * _sigmoid(gate_in) * h[:, EXPERT_FF:]
        y_ref[...] = jnp.dot(act.astype(BF16), wout_ref[0], preferred_element_type=F32)

    @pl.when(c >= nused_ref[0])
    def _():
        y_ref[...] = jnp.zeros(y_ref.shape, F32)


def _moe_ffn(buf, chunk_exp, n_used, w_in, w_out):
    n_chunks = chunk_exp.shape[0]
    grid_spec = pltpu.PrefetchScalarGridSpec(
        num_scalar_prefetch=2,
        grid=(n_chunks,),
        in_specs=[pl.BlockSpec((MOE_CHUNK, D_MODEL), lambda c, ce, nu: (c, 0)),
                  pl.BlockSpec((1, D_MODEL, 2 * EXPERT_FF), lambda c, ce, nu: (ce[c], 0, 0)),
                  pl.BlockSpec((1, EXPERT_FF, D_MODEL), lambda c, ce, nu: (ce[c], 0, 0))],
        out_specs=pl.BlockSpec((MOE_CHUNK, D_MODEL), lambda c, ce, nu: (c, 0)),
    )
    return pl.pallas_call(
        _moe_ffn_kernel,
        grid_spec=grid_spec,
        out_shape=jax.ShapeDtypeStruct(buf.shape, F32),
        compiler_params=_cparams(("arbitrary",)),
        name="moe_ffn",
    )(chunk_exp, n_used, buf, w_in, w_out)


def _moe_combine_kernel(slot_ref, x_ref, gate_ref, g_ref, b_ref, yb_ref, o_ref, rows_ref, sem, *, tc):
    def row_copy(tok, k):
        return pltpu.make_async_copy(yb_ref.at[pl.ds(slot_ref[0, k, tok], 1)], rows_ref.at[k, pl.ds(tok, 1)], sem)

    def start(tok, _):
        for k in range(TOPK_EXPERTS):
            row_copy(tok, k).start()
        return 0

    def wait(tok, _):
        for k in range(TOPK_EXPERTS):
            row_copy(tok, k).wait()
        return 0

    lax.fori_loop(0, tc, start, 0, unroll=8)
    lax.fori_loop(0, tc, wait, 0, unroll=8)
    gates = gate_ref[...]
    y = rows_ref[0] * gates[:, 0:1] + rows_ref[1] * gates[:, 1:2]
    o_ref[...] = _layer_norm(DEEPNORM_ALPHA * x_ref[...] + y, g_ref[...], b_ref[...])


def _moe_combine(x1, gates, slots, yb, ln_g, ln_b, tc=512):
    t = x1.shape[0]
    g = ln_g.reshape(1, -1)
    b = ln_b.reshape(1, -1)
    vec = pl.BlockSpec((1, D_MODEL), lambda i: (0, 0))
    return pl.pallas_call(
        functools.partial(_moe_combine_kernel, tc=tc),
        grid=(t // tc,),
        in_specs=[pl.BlockSpec((1, TOPK_EXPERTS, tc), lambda i: (i, 0, 0), memory_space=pltpu.SMEM),
                  pl.BlockSpec((tc, D_MODEL), lambda i: (i, 0)),
                  pl.BlockSpec((tc, LANES), lambda i: (i, 0)),
                  vec, vec,
                  pl.BlockSpec(memory_space=pl.ANY)],
        out_specs=pl.BlockSpec((tc, D_MODEL), lambda i: (i, 0)),
        out_shape=jax.ShapeDtypeStruct((t, D_MODEL), F32),
        scratch_shapes=[pltpu.VMEM((TOPK_EXPERTS, tc, D_MODEL), F32), pltpu.SemaphoreType.DMA(())],
        compiler_params=_cparams(("arbitrary",)),
        name="moe_combine",
    )(slots, x1, gates, g, b, yb)


def _moe(x1, esel, gate_t, w_in, w_out, ln_g, ln_b, tile=512):
    t = x1.shape[0]
    tile = min(tile, t)
    n_chunks = -(-t * TOPK_EXPERTS // MOE_CHUNK) + N_EXPERTS
    rank, cnt = _moe_rank(esel)
    counts = cnt[:, 0].astype(I32)
    padded = (counts + MOE_CHUNK - 1) // MOE_CHUNK * MOE_CHUNK
    pad_end = jnp.cumsum(padded)
    pad_start = pad_end - padded
    expert = jnp.arange(N_EXPERTS, dtype=I32)
    slot = rank + jnp.sum(jnp.where(esel[None] == expert[:, None, None], pad_start[:, None, None], 0), axis=0)
    slots = slot.reshape(TOPK_EXPERTS, t // tile, tile).transpose(1, 0, 2)
    chunk_lo = jnp.arange(n_chunks, dtype=I32) * MOE_CHUNK
    chunk_exp = jnp.minimum(jnp.sum((pad_end[None, :] <= chunk_lo[:, None]).astype(I32), axis=1), N_EXPERTS - 1)
    n_used = (pad_end[-1] // MOE_CHUNK).astype(I32).reshape(1)
    is_last = jnp.any((chunk_lo[:, None] + MOE_CHUNK) == pad_end[None, :], axis=1)
    zero_flags = (is_last | (chunk_lo >= pad_end[-1])).astype(I32)
    buf = _moe_scatter(x1, slots, zero_flags, n_chunks, ts=tile)
    yb = _moe_ffn(buf, chunk_exp, n_used, w_in, w_out)
    gates = _pad_cols(gate_t.T, LANES)
    return _moe_combine(x1, gates, slots, yb, ln_g, ln_b, tc=tile)


def _pad_cols(w, width, offset=0):
    out = jnp.zeros((w.shape[0], width), w.dtype)
    return out.at[:, offset:offset + w.shape[1]].set(w)


def _rope_rows(half, lane_lo, lane_hi):
    inv_freq = ROPE_THETA ** (-jnp.arange(half, dtype=F32) / half)
    lane = jnp.arange(LANES)
    active = (lane >= lane_lo) & (lane < lane_hi)
    pos_in = (lane - lane_lo) % (2 * half)
    freq = jnp.where(active, inv_freq[pos_in % half], 0.0).astype(F32)
    sign = jnp.where(active, jnp.where(pos_in < half, -1.0, 1.0), 0.0).astype(F32)
    return freq.reshape(1, LANES), sign.reshape(1, LANES)


def kernel(x, mem, positions, router_w, router_bias, l0_w_in, l0_mem_wkv, l0_w_o, l0_ln_g, l0_ln_b, l0_moe_w_in, l0_moe_w_out, l1_w_in, l1_conv_w, l1_conv_b, l1_conv_norm_g, l1_conv_norm_b, l1_mem_wkv, l1_w_o, l1_ln_g, l1_ln_b, l1_moe_w_in, l1_moe_w_out, l2_w_in, l2_mem_wkv, l2_w_o, l2_ln_g, l2_ln_b, l2_moe_w_in, l2_moe_w_out, l3_w_in, l3_q_norm_g, l3_kv_norm_g, l3_w_q_up, l3_w_kv_up, l3_mem_wkv, l3_w_o, l3_ln_g, l3_ln_b, l3_moe_w_in, l3_moe_w_out):
    bsz, seq, d = x.shape
    t = bsz * seq
    n_mem = mem.shape[1]
    xt = x.reshape(t, d)
    mem2 = mem.reshape(bsz * n_mem, d)
    pos_b = jnp.broadcast_to(positions.astype(F32).reshape(t, 1), (t, LANES))
    rw_t = router_w.T.astype(BF16)
    rbias = router_bias.astype(F32).reshape(N_EXPERTS, 1)

    def finish_layer(xt, mix, mq, mem_wkv, w_o, ln_g, ln_b, moe_w_in, moe_w_out):
        (kvm,) = _proj(mem2, mem_wkv.astype(BF16), [(0, 2 * MEM_WIDTH, 2 * MEM_WIDTH, 0)], [BF16],
                       tm=n_mem, name="mem_kv")
        kvm = kvm.reshape(bsz, n_mem, 2 * MEM_WIDTH)
        x1, esel, gate_t = _post_mixer(xt, mix, mq, kvm, w_o.astype(BF16), ln_g[0], ln_b[0], rw_t, rbias, seq)
        return _moe(x1, esel, gate_t, moe_w_in.astype(BF16), moe_w_out.astype(BF16), ln_g[1], ln_b[1])

    w0 = l0_w_in.at[:, :SB_HEADS * LANES].multiply(LANES ** -0.5).astype(BF16)
    n0 = w0.shape[1]
    (h0,) = _proj(xt, w0, [(0, n0, n0, 0)], [BF16], name="proj0")
    mix = _sb_attention(h0, bsz, seq)
    mq = h0[:, 3 * D_MODEL:]
    xt = finish_layer(xt, mix, mq, l0_mem_wkv, l0_w_o, l0_ln_g, l0_ln_b, l0_moe_w_in, l0_moe_w_out)

    w1 = l1_w_in.astype(BF16)
    a, g, mq = _proj(xt, w1, [(0, D_MODEL, D_MODEL, 0), (D_MODEL, D_MODEL, D_MODEL, 0),
                              (2 * D_MODEL, MEM_WIDTH, MEM_WIDTH, 0)], [F32, F32, BF16], name="proj1")
    mix = _conformer_conv(a, g, l1_conv_w, l1_conv_b, l1_conv_norm_g, l1_conv_norm_b, seq)
    xt = finish_layer(xt, mix, mq, l1_mem_wkv, l1_w_o, l1_ln_g, l1_ln_b, l1_moe_w_in, l1_moe_w_out)

    w2 = l2_w_in
    c = 0
    parts = []
    for width in (DSA_HEADS * DSA_HEAD_DIM, DSA_HEAD_DIM, DSA_HEAD_DIM, IDX_HEADS * IDX_DIM, IDX_DIM, IDX_HEADS,
                  MEM_WIDTH):
        parts.append(w2[:, c:c + width])
        c += width
    wq, wk, wv, wqi, wki, wwi, wmq = parts
    w2p = jnp.concatenate([wq, _pad_cols(wk, LANES), _pad_cols(wv, LANES), wqi, _pad_cols(wki, LANES),
                           _pad_cols(wwi, LANES), wmq], axis=1).astype(BF16)
    freq, sign = _rope_rows(DSA_HEAD_DIM // 2, 0, LANES)
    cos_a, sin_a = _rope_tables(pos_b, freq, sign)
    rh = DSA_HEAD_DIM // 2
    groups = [(0, 1024, 1024, rh), (1024, LANES, DSA_HEAD_DIM, rh), (1152, LANES, LANES, 0),
              (1280, 512, 512, rh), (1792, LANES, IDX_DIM, rh), (1920, LANES, LANES, 0),
              (2048, MEM_WIDTH, MEM_WIDTH, 0)]
    q, k, v, qi, ki, wi, mq = _proj(xt, w2p, groups, [BF16, BF16, BF16, BF16, BF16, F32, BF16],
                                    rope=(cos_a, sin_a), name="proj2")
    mix = _dsa_attention(q, qi, wi, k, v, ki, bsz, seq)
    xt = finish_layer(xt, mix, mq, l2_mem_wkv, l2_w_o, l2_ln_g, l2_ln_b, l2_moe_w_in, l2_moe_w_out)

    w3 = l3_w_in
    rope_lo = MLA_NOPE
    w_cq = w3[:, :MLA_Q_RANK]
    w_ckv = w3[:, MLA_Q_RANK:MLA_Q_RANK + MLA_KV_RANK]
    w_kr = w3[:, MLA_Q_RANK + MLA_KV_RANK:MLA_Q_RANK + MLA_KV_RANK + MLA_ROPE]
    w_mq = w3[:, MLA_Q_RANK + MLA_KV_RANK + MLA_ROPE:]
    w3p = jnp.concatenate([w_cq, w_ckv, _pad_cols(w_kr, LANES, rope_lo), w_mq], axis=1).astype(BF16)
    freq, sign = _rope_rows(MLA_ROPE // 2, rope_lo, rope_lo + MLA_ROPE)
    cos_b, sin_b = _rope_tables(pos_b, freq, sign)
    groups = [(0, MLA_Q_RANK, MLA_Q_RANK, 0), (MLA_Q_RANK, MLA_KV_RANK, MLA_KV_RANK, 0),
              (MLA_Q_RANK + MLA_KV_RANK, LANES, LANES, MLA_ROPE // 2),
              (MLA_Q_RANK + MLA_KV_RANK + LANES, MEM_WIDTH, MEM_WIDTH, 0)]
    c_q, c_kv, kr, mq = _proj(xt, w3p, groups, [F32, F32, F32, BF16], rope=(cos_b, sin_b), name="proj3")
    qk_dim = MLA_NOPE + MLA_ROPE
    wq_heads = l3_w_q_up.reshape(MLA_Q_RANK, MLA_HEADS, qk_dim)
    wq_p = jnp.zeros((MLA_Q_RANK, MLA_HEADS, LANES), F32).at[:, :, :qk_dim].set(wq_heads * qk_dim ** -0.5)
    wkv_heads = l3_w_kv_up.reshape(MLA_KV_RANK, MLA_HEADS, MLA_NOPE + MLA_V)
    wk_p = jnp.zeros((MLA_KV_RANK, MLA_HEADS, LANES), F32).at[:, :, :MLA_NOPE].set(wkv_heads[:, :, :MLA_NOPE])
    wv_p = wkv_heads[:, :, MLA_NOPE:]
    flat = lambda w: w.reshape(w.shape[0], MLA_HEADS * LANES).astype(BF16)
    qh, kh, vh = _mla_prep(c_q, c_kv, kr, cos_b, sin_b, l3_q_norm_g, l3_kv_norm_g, flat(wq_p), flat(wk_p),
                           flat(wv_p))
    mix = _causal_attention(qh, kh, vh, bsz, seq, MLA_HEADS)
    xt = finish_layer(xt, mix, mq, l3_mem_wkv, l3_w_o, l3_ln_g, l3_ln_b, l3_moe_w_in, l3_moe_w_out)
    return xt.reshape(bsz, seq, d)
```
